```python
import math
import jax, jax.numpy as jnp
from jax import lax
import numpy as np

D_MODEL = 1024
BATCH = 4
SEQ = 8192
DEPTH = 4

CHUNK = 64
Q_BLOCK = 128
EPS = 1e-6

FOX_HEADS = 4
FOX_DIM = 128
GDN_HEADS = 4
GDN_DK = 128
GDN_DV = 128
GDN_CONV = 4
SB_HEADS = 4
SB_DIM = 128
MEM_TOKENS = 256
MEM_HEADS = 4
MEM_DIM = 128
D_FF = 2816
FFN_CONV = 3
N_BRANCH = 3

FOX_W = FOX_HEADS * FOX_DIM
GDN_KW = GDN_HEADS * GDN_DK
GDN_VW = GDN_HEADS * GDN_DV
SB_W = SB_HEADS * SB_DIM
MEM_W = MEM_HEADS * MEM_DIM

IN_SPLITS = (
    FOX_W, FOX_W, FOX_W, FOX_HEADS,
    GDN_KW, GDN_KW, GDN_VW, GDN_HEADS, GDN_HEADS, GDN_VW,
    SB_W, SB_W, SB_W,
    N_BRANCH * D_MODEL,
)
N_IN = sum(IN_SPLITS)

kernel_name = "hybrid_fox_gdn_stickbreak_encoder"


def rmsnorm(x, g):
    xf = x.astype(jnp.float32)
    y = xf * lax.rsqrt(jnp.mean(xf * xf, axis=-1, keepdims=True) + EPS)
    return (y * g.astype(jnp.float32)).astype(x.dtype)


def l2norm(x):
    xf = x.astype(jnp.float32)
    return xf * lax.rsqrt(jnp.sum(xf * xf, axis=-1, keepdims=True) + EPS)


def heads(x, n):
    return x.reshape(x.shape[:-1] + (n, -1))


def causal_dwconv(x, w):
    width, ch = w.shape
    return lax.conv_general_dilated(
        x, w[:, None, :].astype(x.dtype), window_strides=(1,), padding=[(width - 1, 0)],
        dimension_numbers=("NWC", "WIO", "NWC"), feature_group_count=ch)


def split_projection(p):
    points = [int(s) for s in np.cumsum(IN_SPLITS)[:-1]]
    return jnp.split(p, points, axis=-1)


def fox_attention(q, k, v, logf):
    B, S, H, dh = q.shape
    c = jnp.cumsum(logf, axis=1).transpose(0, 2, 1)
    qh = q.transpose(0, 2, 1, 3)
    kh = k.transpose(0, 2, 1, 3)
    vh = v.transpose(0, 2, 1, 3)
    scale = dh ** -0.5
    outs = []
    for i in range(S // Q_BLOCK):
        lo, hi = i * Q_BLOCK, (i + 1) * Q_BLOCK
        s = (jnp.einsum("bhqd,bhkd->bhqk", qh[:, :, lo:hi], kh[:, :, :hi]).astype(jnp.float32) * scale
             + (c[:, :, lo:hi, None] - c[:, :, None, :hi]))
        mask = jnp.arange(hi)[None, :] <= jnp.arange(lo, hi)[:, None]
        p = jax.nn.softmax(jnp.where(mask, s, -jnp.inf), axis=-1).astype(vh.dtype)
        outs.append(jnp.einsum("bhqk,bhkd->bhqd", p, vh[:, :, :hi]))
    o = jnp.concatenate(outs, axis=2)
    return o.transpose(0, 2, 1, 3).reshape(B, S, H * dh)


def stick_breaking_attention(q, k, v):
    B, S, H, dh = q.shape
    qh = q.transpose(0, 2, 1, 3)
    kh = k.transpose(0, 2, 1, 3)
    vh = v.transpose(0, 2, 1, 3)
    scale = dh ** -0.5
    idx = jnp.arange(Q_BLOCK)
    tri_in = (idx[:, None] >= idx[None, :]).astype(jnp.float32)
    outs = []
    for i in range(S // Q_BLOCK):
        lo, hi = i * Q_BLOCK, (i + 1) * Q_BLOCK
        nk = i + 1
        z = jnp.einsum("bhqd,bhkd->bhqk", qh[:, :, lo:hi], kh[:, :, :hi]).astype(jnp.float32) * scale
        mask = jnp.arange(hi)[None, :] < jnp.arange(lo, hi)[:, None]
        log_keep = jnp.where(mask, -jax.nn.softplus(z), 0.0)
        lk = log_keep.reshape(B, H, Q_BLOCK, nk, Q_BLOCK)
        within = jnp.einsum("bhqnj,jk->bhqnk", lk, tri_in)
        tot = jnp.sum(lk, axis=-1)
        blk = jnp.arange(nk)
        tri_blk = (blk[:, None] > blk[None, :]).astype(jnp.float32)
        after = jnp.einsum("bhqm,mn->bhqn", tot, tri_blk)
        rc = (within + after[..., None]).reshape(B, H, Q_BLOCK, hi)
        a = jnp.exp(jnp.where(mask, z + rc, -jnp.inf)).astype(vh.dtype)
        outs.append(jnp.einsum("bhqk,bhkd->bhqd", a, vh[:, :, :hi]))
    o = jnp.concatenate(outs, axis=2)
    return o.transpose(0, 2, 1, 3).reshape(B, S, H * dh)


def gated_delta_rule(q, k, v, g, beta):
    B, T, H, dk = q.shape
    dv = v.shape[-1]
    N = T // CHUNK
    f32 = jnp.float32

    def chunks(a):
        a = a.astype(f32).reshape((B, N, CHUNK, H) + a.shape[3:])
        return jnp.moveaxis(a, (1, 3), (0, 2))

    qc = chunks(q) * dk ** -0.5
    kc = chunks(k)
    vc = chunks(v)
    bc = chunks(beta)
    gc = jnp.cumsum(chunks(g), axis=-1)
    idx = jnp.arange(CHUNK)
    causal = idx[:, None] >= idx[None, :]
    strict = idx[:, None] > idx[None, :]
    decay = jnp.exp(jnp.where(causal, gc[..., :, None] - gc[..., None, :], -jnp.inf))
    kk = jnp.einsum("nbhcd,nbhed->nbhce", kc, kc)
    a_mat = jnp.where(strict, bc[..., :, None] * kk * decay, 0.0) + jnp.eye(CHUNK, dtype=f32)
    rhs = jnp.concatenate([vc * bc[..., None], kc * (bc * jnp.exp(gc))[..., None]], axis=-1)
    sol = lax.linalg.triangular_solve(a_mat, rhs, left_side=True, lower=True)
    u, w = sol[..., :dv], sol[..., dv:]
    attn = jnp.where(causal, jnp.einsum("nbhcd,nbhed->nbhce", qc, kc) * decay, 0.0)
    g_last = gc[..., -1]
    k_tail = kc * jnp.exp(g_last[..., None] - gc)[..., None]

    def step(state, xs):
        q_n, u_n, w_n, g_n, attn_n, kt_n, gl_n = xs
        v_new = u_n - jnp.einsum("bhck,bhkv->bhcv", w_n, state)
        o = (jnp.einsum("bhck,bhkv->bhcv", q_n * jnp.exp(g_n)[..., None], state)
             + jnp.einsum("bhce,bhev->bhcv", attn_n, v_new))
        state = state * jnp.exp(gl_n)[..., None, None] + jnp.einsum("bhck,bhcv->bhkv", kt_n, v_new)
        return state, o

    s0 = jnp.zeros((B, H, dk, dv), f32)
    _, o = lax.scan(step, s0, (qc, u, w, gc, attn, k_tail, g_last))
    return jnp.moveaxis(o, (0, 2), (1, 3)).reshape(B, T, H, dv)


def memory_cross_attention(h, m, w_q, w_kv, q_g, k_g, w_o):
    B, S, _ = h.shape
    q = rmsnorm(heads(h @ w_q, MEM_HEADS), q_g)
    k, v = jnp.split(m @ w_kv, 2, axis=-1)
    k = rmsnorm(heads(k, MEM_HEADS), k_g)
    v = heads(v, MEM_HEADS)
    s = jnp.einsum("bqhd,bkhd->bhqk", q, k).astype(jnp.float32) * MEM_DIM ** -0.5
    p = jax.nn.softmax(s, axis=-1).astype(v.dtype)
    o = jnp.einsum("bhqk,bkhd->bqhd", p, v).reshape(B, S, MEM_W)
    return o @ w_o


def conv_ffn(h, w_up, conv_w, conv_b, w_down):
    u = causal_dwconv(h @ w_up, conv_w) + conv_b
    a, b = jnp.split(u, 2, axis=-1)
    return (jax.nn.silu(a) * b) @ w_down


def setup_inputs(seed: int = 0) -> dict:
    key = jax.random.key(seed)
    ks = iter(jax.random.split(key, 40))
    L, D = DEPTH, D_MODEL
    f32 = jnp.float32

    def nrm(shape, scale):
        return jax.random.normal(next(ks), shape, f32) * scale

    def gain(shape):
        return 1.0 + 0.02 * jax.random.normal(next(ks), shape, f32)

    x = nrm((BATCH, SEQ, D), 1.0)
    mem = nrm((BATCH, MEM_TOKENS, D), 1.0)
    norm_mix = gain((L, D))
    w_in = nrm((L, D, N_IN), D ** -0.5)
    fox_fbias = jax.random.uniform(next(ks), (L, FOX_HEADS), f32, minval=1.0, maxval=5.0)
    fox_qnorm = gain((L, FOX_DIM))
    fox_knorm = gain((L, FOX_DIM))
    gdn_conv = nrm((L, GDN_CONV, 2 * GDN_KW + GDN_VW), GDN_CONV ** -0.5)
    gdn_a_log = jnp.log(jax.random.uniform(next(ks), (L, GDN_HEADS), f32, minval=1.0, maxval=16.0))
    dt = jnp.exp(jax.random.uniform(next(ks), (L, GDN_HEADS), f32,
                                    minval=math.log(1e-3), maxval=math.log(1e-1)))
    gdn_dt_bias = dt + jnp.log(-jnp.expm1(-dt))
    gdn_onorm = gain((L, GDN_DV))
    gate_bias = nrm((L, N_BRANCH * D), 0.01)
    w_oa = nrm((L, FOX_W, D), FOX_W ** -0.5)
    w_ob = nrm((L, GDN_VW, D), GDN_VW ** -0.5)
    w_oc = nrm((L, SB_W, D), SB_W ** -0.5)
    w_out = nrm((L, D, D), D ** -0.5)
    norm_xq = gain((L, D))
    norm_mem = gain((L, D))
    w_mq = nrm((L, D, MEM_W), D ** -0.5)
    w_mkv = nrm((L, D, 2 * MEM_W), D ** -0.5)
    mq_norm = gain((L, MEM_DIM))
    mk_norm = gain((L, MEM_DIM))
    w_mo = nrm((L, MEM_W, D), MEM_W ** -0.5)
    norm_ffn = gain((L, D))
    w_up = nrm((L, D, 2 * D_FF), D ** -0.5)
    ffn_conv = nrm((L, FFN_CONV, 2 * D_FF), FFN_CONV ** -0.5)
    ffn_conv_b = nrm((L, 2 * D_FF), 0.01)
    w_down = nrm((L, D_FF, D), D_FF ** -0.5)
    return {
        "x": x, "mem": mem, "norm_mix": norm_mix, "w_in": w_in,
        "fox_fbias": fox_fbias, "fox_qnorm": fox_qnorm, "fox_knorm": fox_knorm,
        "gdn_conv": gdn_conv, "gdn_a_log": gdn_a_log, "gdn_dt_bias": gdn_dt_bias,
        "gdn_onorm": gdn_onorm, "gate_bias": gate_bias,
        "w_oa": w_oa, "w_ob": w_ob, "w_oc": w_oc, "w_out": w_out,
        "norm_xq": norm_xq, "norm_mem": norm_mem, "w_mq": w_mq, "w_mkv": w_mkv,
        "mq_norm": mq_norm, "mk_norm": mk_norm, "w_mo": w_mo,
        "norm_ffn": norm_ffn, "w_up": w_up, "ffn_conv": ffn_conv,
        "ffn_conv_b": ffn_conv_b, "w_down": w_down,
    }


def reference(x, mem, norm_mix, w_in, fox_fbias, fox_qnorm, fox_knorm, gdn_conv, gdn_a_log,
              gdn_dt_bias, gdn_onorm, gate_bias, w_oa, w_ob, w_oc, w_out, norm_xq, norm_mem,
              w_mq, w_mkv, mq_norm, mk_norm, w_mo, norm_ffn, w_up, ffn_conv, ffn_conv_b, w_down):
    B, S, D = x.shape
    f32 = jnp.float32
    for l in range(DEPTH):
        h = rmsnorm(x, norm_mix[l])
        (fq, fk, fv, ff, gq, gk, gv, gb, ga, gz, sq, sk, sv, gates) = split_projection(h @ w_in[l])

        fq = rmsnorm(heads(fq, FOX_HEADS), fox_qnorm[l])
        fk = rmsnorm(heads(fk, FOX_HEADS), fox_knorm[l])
        logf = jax.nn.log_sigmoid((ff + fox_fbias[l]).astype(f32))
        ya = fox_attention(fq, fk, heads(fv, FOX_HEADS), logf)

        qkv = jax.nn.silu(causal_dwconv(jnp.concatenate([gq, gk, gv], axis=-1), gdn_conv[l]))
        cq, ck, cv = jnp.split(qkv, [GDN_KW, 2 * GDN_KW], axis=-1)
        beta = jax.nn.sigmoid(gb.astype(f32))
        g_log = -jnp.exp(gdn_a_log[l].astype(f32)) * jax.nn.softplus((ga + gdn_dt_bias[l]).astype(f32))
        o = gated_delta_rule(l2norm(heads(cq, GDN_HEADS)), l2norm(heads(ck, GDN_HEADS)),
                             heads(cv, GDN_HEADS), g_log, beta)
        yb = (rmsnorm(o, gdn_onorm[l]) * jax.nn.silu(heads(gz, GDN_HEADS).astype(f32)))
        yb = yb.astype(x.dtype).reshape(B, S, GDN_VW)

        yc = stick_breaking_attention(heads(sq, SB_HEADS), heads(sk, SB_HEADS), heads(sv, SB_HEADS))

        g = jax.nn.sigmoid((gates + gate_bias[l]).astype(f32)).astype(x.dtype).reshape(B, S, N_BRANCH, D)
        mixed = (g[..., 0, :] * (ya @ w_oa[l]) + g[..., 1, :] * (yb @ w_ob[l])
                 + g[..., 2, :] * (yc @ w_oc[l]))
        x = x + mixed @ w_out[l]

        x = x + memory_cross_attention(rmsnorm(x, norm_xq[l]), rmsnorm(mem, norm_mem[l]),
                                       w_mq[l], w_mkv[l], mq_norm[l], mk_norm[l], w_mo[l])

        x = x + conv_ffn(rmsnorm(x, norm_ffn[l]), w_up[l], ffn_conv[l], ffn_conv_b[l], w_down[l])
    return x
```

```python
import functools

import jax
import jax.numpy as jnp
from jax import lax
from jax.experimental import pallas as pl
from jax.experimental.pallas import tpu as pltpu

F32 = jnp.float32
BF16 = jnp.bfloat16
EPS = 1e-6

HEADS = 4
HEAD_DIM = 128
MIX_W = HEADS * HEAD_DIM
GDN_CHUNK = 64
GDN_CONV = 4
FFN_CONV = 3
LANES = 128
SUBLANES = 8
VMEM_LIMIT = 56 * 1024 * 1024

COL_FQ, COL_FK, COL_FV = 0, 512, 1024
COL_GQ, COL_GK, COL_GV, COL_GZ = 1536, 2048, 2560, 3072
COL_SQ, COL_SK, COL_SV = 3584, 4096, 4608
COL_GATES = 5120
N_BIG = 8192
LANE_FF, LANE_GB, LANE_GA = 0, 4, 8

NT_DIMS = (((1,), (1,)), ((), ()))
TN_DIMS = (((0,), (0,)), ((), ()))


def _cparams(*sem):
    return pltpu.CompilerParams(dimension_semantics=sem, vmem_limit_bytes=VMEM_LIMIT)


def _const_spec(shape):
    nd = len(shape)
    return pl.BlockSpec(shape, lambda *_: (0,) * nd, pipeline_mode=pl.Buffered(1))


def _rms(xf, gain):
    return xf * lax.rsqrt(jnp.mean(xf * xf, axis=-1, keepdims=True) + EPS) * gain


def _softplus(x):
    return jnp.maximum(x, 0.0) + jnp.log1p(jnp.exp(-jnp.abs(x)))


def _silu(x):
    return x * jax.nn.sigmoid(x)


def _split3(x):
    a = x.astype(BF16)
    r = x - a.astype(F32)
    b = r.astype(BF16)
    c = (r - b.astype(F32)).astype(BF16)
    return a, b, c


def _lane_col(x, lane):
    idx = lax.broadcasted_iota(jnp.int32, x.shape, 1)
    return jnp.sum(jnp.where(idx == lane, x, 0.0), axis=-1, keepdims=True)


def _inproj_kernel(x_ref, g_ref, w_ref, ws_ref, y_ref, gs_ref, *, n_chunk):
    h = _rms(x_ref[...], g_ref[...]).astype(BF16)
    for c in range(0, N_BIG, n_chunk):
        y_ref[:, c:c + n_chunk] = jnp.dot(
            h, w_ref[:, c:c + n_chunk], preferred_element_type=F32).astype(BF16)
    gs_ref[...] = jnp.dot(h, ws_ref[...], preferred_element_type=F32)


def _inproj(xf, gain, w_big, w_small, tm):
    m, d = xf.shape
    return pl.pallas_call(
        functools.partial(_inproj_kernel, n_chunk=512),
        grid=(m // tm,),
        in_specs=[
            pl.BlockSpec((tm, d), lambda i: (i, 0)),
            _const_spec((1, d)),
            _const_spec((d, N_BIG)),
            _const_spec((d, LANES)),
        ],
        out_specs=[
            pl.BlockSpec((tm, N_BIG), lambda i: (i, 0)),
            pl.BlockSpec((tm, LANES), lambda i: (i, 0)),
        ],
        out_shape=[jax.ShapeDtypeStruct((m, N_BIG), BF16),
                   jax.ShapeDtypeStruct((m, LANES), F32)],
        compiler_params=_cparams("parallel"),
        name="inproj",
    )(xf, gain, w_big, w_small)


def _fox_prep_kernel(yq_ref, yk_ref, g_ref, fb_ref, qn_ref, kn_ref, qa_ref, ka_ref, carry_ref, *, t):
    @pl.when(pl.program_id(1) == 0)
    def _():
        carry_ref[...] = jnp.zeros_like(carry_ref)

    lane = lax.broadcasted_iota(jnp.int32, (t, LANES), 1)
    x = g_ref[...] + fb_ref[...]
    logf = jnp.minimum(x, 0.0) - jnp.log1p(jnp.exp(-jnp.abs(x)))
    logf = jnp.where(lane < LANE_FF + HEADS, logf, 0.0)
    row = lax.broadcasted_iota(jnp.int32, (t, t), 0)
    col = lax.broadcasted_iota(jnp.int32, (t, t), 1)
    tri = jnp.where(row >= col, 1.0, 0.0).astype(BF16)
    l1, l2, l3 = _split3(logf)
    c_all = (jnp.dot(tri, l1, preferred_element_type=F32)
             + jnp.dot(tri, l2, preferred_element_type=F32)
             + jnp.dot(tri, l3, preferred_element_type=F32)) + carry_ref[0:1, :]
    carry_ref[0:1, :] = c_all[t - 1:t, :]

    scale = HEAD_DIM ** -0.5
    for h in range(HEADS):
        sl = slice(h * HEAD_DIM, (h + 1) * HEAD_DIM)
        qn = _rms(yq_ref[:, sl].astype(F32), qn_ref[...]) * scale
        kn = _rms(yk_ref[:, sl].astype(F32), kn_ref[...])
        cb = jnp.broadcast_to(_lane_col(c_all, LANE_FF + h), (t, LANES))
        c1, c2, c3 = (p.astype(F32) for p in _split3(cb))
        aux_q = jnp.where(lane == 0, c1, jnp.where(lane == 1, c2, jnp.where(
            lane == 2, c3, jnp.where(lane < 6, 1.0, 0.0))))
        aux_k = jnp.where(lane < 3, 1.0, jnp.where(lane == 3, -c1, jnp.where(
            lane == 4, -c2, jnp.where(lane == 5, -c3, 0.0))))
        base = 2 * h * HEAD_DIM
        qa_ref[:, base:base + HEAD_DIM] = qn.astype(BF16)
        qa_ref[:, base + HEAD_DIM:base + 2 * HEAD_DIM] = aux_q.astype(BF16)
        ka_ref[:, base:base + HEAD_DIM] = kn.astype(BF16)
        ka_ref[:, base + HEAD_DIM:base + 2 * HEAD_DIM] = aux_k.astype(BF16)


def _fox_prep(y, g, fb_row, qn, kn, batch, seq, t):
    m = batch * seq
    nt = seq // t
    rows = lambda b, i: b * nt + i
    return pl.pallas_call(
        functools.partial(_fox_prep_kernel, t=t),
        grid=(batch, nt),
        in_specs=[
            pl.BlockSpec((t, MIX_W), lambda b, i: (rows(b, i), COL_FQ // MIX_W)),
            pl.BlockSpec((t, MIX_W), lambda b, i: (rows(b, i), COL_FK // MIX_W)),
            pl.BlockSpec((t, LANES), lambda b, i: (rows(b, i), 0)),
            _const_spec((1, LANES)),
            _const_spec((1, HEAD_DIM)),
            _const_spec((1, HEAD_DIM)),
        ],
        out_specs=[
            pl.BlockSpec((t, 2 * MIX_W), lambda b, i: (rows(b, i), 0)),
            pl.BlockSpec((t, 2 * MIX_W), lambda b, i: (rows(b, i), 0)),
        ],
        out_shape=[jax.ShapeDtypeStruct((m, 2 * MIX_W), BF16)] * 2,
        scratch_shapes=[pltpu.VMEM((SUBLANES, LANES), F32)],
        compiler_params=_cparams("parallel", "arbitrary"),
        name="fox_prep",
    )(y, y, g, fb_row, qn, kn)


def _fox_attn_kernel(q_ref, k_ref, v_ref, o_ref, *, tq):
    i = pl.program_id(2)
    q = q_ref[...]

    def tile(j, carry, masked):
        m_i, l_i, acc = carry
        start = pl.multiple_of(j * tq, tq)
        k = k_ref[pl.ds(start, tq), :]
        v = v_ref[pl.ds(start, tq), :]
        s = lax.dot_general(q, k, NT_DIMS, preferred_element_type=F32)
        if masked:
            row = lax.broadcasted_iota(jnp.int32, (tq, tq), 0)
            col = lax.broadcasted_iota(jnp.int32, (tq, tq), 1)
            s = jnp.where(col <= row, s, -jnp.inf)
        m_new = jnp.maximum(m_i, jnp.max(s, axis=-1, keepdims=True))
        p = jnp.exp(s - m_new)
        alpha = jnp.exp(m_i - m_new)
        l_new = alpha * l_i + jnp.sum(p, axis=-1, keepdims=True)
        acc = alpha * acc + jnp.dot(p.astype(BF16), v, preferred_element_type=F32)
        return m_new, l_new, acc

    init = (jnp.full((tq, 1), -jnp.inf, F32), jnp.zeros((tq, 1), F32),
            jnp.zeros((tq, HEAD_DIM), F32))
    carry = lax.fori_loop(0, i, lambda j, c: tile(j, c, False), init)
    _, l_i, acc = tile(i, carry, True)
    o_ref[...] = (acc / l_i).astype(BF16)


def _fox_attn(q_aug, k_aug, y, batch, seq, tq):
    m = batch * seq
    nq = seq // tq
    return pl.pallas_call(
        functools.partial(_fox_attn_kernel, tq=tq),
        grid=(batch, HEADS, nq),
        in_specs=[
            pl.BlockSpec((tq, 2 * HEAD_DIM), lambda b, h, i: (b * nq + i, h)),
            pl.BlockSpec((seq, 2 * HEAD_DIM), lambda b, h, i: (b, h)),
            pl.BlockSpec((seq, HEAD_DIM), lambda b, h, i: (b, COL_FV // HEAD_DIM + h)),
        ],
        out_specs=pl.BlockSpec((tq, HEAD_DIM), lambda b, h, i: (b * nq + i, h)),
        out_shape=jax.ShapeDtypeStruct((m, MIX_W), BF16),
        compiler_params=_cparams("parallel", "parallel", "arbitrary"),
        name="fox_attn",
    )(q_aug, k_aug, y)


def _sb_attn_kernel(q_ref, k_ref, v_ref, o_ref, *, tq):
    i = pl.program_id(2)
    q = q_ref[...]
    scale = HEAD_DIM ** -0.5
    row = lax.broadcasted_iota(jnp.int32, (tq, tq), 0)
    col = lax.broadcasted_iota(jnp.int32, (tq, tq), 1)
    suffix = jnp.where(row >= col, 1.0, 0.0).astype(BF16)

    def tile(j, carry, masked):
        after, acc = carry
        start = pl.multiple_of(j * tq, tq)
        k = k_ref[pl.ds(start, tq), :]
        v = v_ref[pl.ds(start, tq), :]
        z = lax.dot_general(q, k, NT_DIMS, preferred_element_type=F32) * scale
        lk = -_softplus(z)
        if masked:
            valid = col < row
            lk = jnp.where(valid, lk, 0.0)
        lk_hi = lk.astype(BF16)
        lk_lo = (lk - lk_hi.astype(F32)).astype(BF16)
        within = (jnp.dot(lk_hi, suffix, preferred_element_type=F32)
                  + jnp.dot(lk_lo, suffix, preferred_element_type=F32))
        e = z + within + after
        if masked:
            e = jnp.where(valid, e, -jnp.inf)
        a = jnp.exp(e)
        acc = acc + jnp.dot(a.astype(BF16), v, preferred_element_type=F32)
        after = after + within[:, 0:1]
        return after, acc

    init = (jnp.zeros((tq, 1), F32), jnp.zeros((tq, HEAD_DIM), F32))
    carry = tile(i, init, True)
    _, acc = lax.fori_loop(0, i, lambda jj, c: tile(i - 1 - jj, c, False), carry)
    o_ref[...] = acc.astype(BF16)


def _sb_attn(y, batch, seq, tq):
    m = batch * seq
    nq = seq // tq
    return pl.pallas_call(
        functools.partial(_sb_attn_kernel, tq=tq),
        grid=(batch, HEADS, nq),
        in_specs=[
            pl.BlockSpec((tq, HEAD_DIM), lambda b, h, i: (b * nq + i, COL_SQ // HEAD_DIM + h)),
            pl.BlockSpec((seq, HEAD_DIM), lambda b, h, i: (b, COL_SK // HEAD_DIM + h)),
            pl.BlockSpec((seq, HEAD_DIM), lambda b, h, i: (b, COL_SV // HEAD_DIM + h)),
        ],
        out_specs=pl.BlockSpec((tq, HEAD_DIM), lambda b, h, i: (b * nq + i, h)),
        out_shape=jax.ShapeDtypeStruct((m, MIX_W), BF16),
        compiler_params=_cparams("parallel", "parallel", "arbitrary"),
        name="sb_attn",
    )(y, y, y)


def _unit_lower_inverse(a_strict, eye):
    hp = lax.Precision.HIGHEST
    x = eye - a_strict
    p = jnp.dot(a_strict, a_strict, precision=hp, preferred_element_type=F32)
    n = a_strict.shape[0]
    power = 2
    while True:
        x = x + jnp.dot(x, p, precision=hp, preferred_element_type=F32)
        power *= 2
        if power >= n:
            break
        p = jnp.dot(p, p, precision=hp, preferred_element_type=F32)
    return x


def _gdn_kernel(yq_ref, yk_ref, yv_ref, yz_ref, g_ref, cw_ref, alog_ref, bias_ref, on_ref,
                o_ref, state_ref, xs_ref, *, tg):
    @pl.when(pl.program_id(1) == 0)
    def _():
        state_ref[...] = jnp.zeros_like(state_ref)
        xs_ref[0:SUBLANES, :] = jnp.zeros((SUBLANES, 3 * MIX_W), F32)

    c = GDN_CHUNK
    hp = lax.Precision.HIGHEST
    xs_ref[SUBLANES:SUBLANES + tg, 0:MIX_W] = yq_ref[...].astype(F32)
    xs_ref[SUBLANES:SUBLANES + tg, MIX_W:2 * MIX_W] = yk_ref[...].astype(F32)
    xs_ref[SUBLANES:SUBLANES + tg, 2 * MIX_W:3 * MIX_W] = yv_ref[...].astype(F32)
    conv = jnp.zeros((tg, 3 * MIX_W), F32)
    for tap in range(GDN_CONV):
        off = SUBLANES - (GDN_CONV - 1) + tap
        conv = conv + cw_ref[tap:tap + 1, :] * xs_ref[off:off + tg, :]
    xs_ref[0:SUBLANES, :] = xs_ref[tg:tg + SUBLANES, :]
    qkv = _silu(conv)

    g_all = g_ref[...]
    beta_all = jax.nn.sigmoid(g_all)
    glog_all = -jnp.exp(alog_ref[...]) * _softplus(g_all + bias_ref[...])
    lane = lax.broadcasted_iota(jnp.int32, (tg, LANES), 1)
    glog_all = jnp.where((lane >= LANE_GA) & (lane < LANE_GA + HEADS), glog_all, 0.0)

    row = lax.broadcasted_iota(jnp.int32, (c, c), 0)
    col = lax.broadcasted_iota(jnp.int32, (c, c), 1)
    causal = row >= col
    strict = row > col
    eye = jnp.where(row == col, 1.0, 0.0)
    tril = jnp.where(causal, 1.0, 0.0).astype(BF16)

    for n in range(tg // c):
        r0 = n * c
        g1, g2, g3 = _split3(glog_all[r0:r0 + c, :])
        gc_all = (jnp.dot(tril, g1, preferred_element_type=F32)
                  + jnp.dot(tril, g2, preferred_element_type=F32)
                  + jnp.dot(tril, g3, preferred_element_type=F32))
        gc_rows = gc_all.T
        for h in range(HEADS):
            sl = slice(h * HEAD_DIM, (h + 1) * HEAD_DIM)
            cq = qkv[r0:r0 + c, h * HEAD_DIM:(h + 1) * HEAD_DIM]
            ck = qkv[r0:r0 + c, MIX_W + h * HEAD_DIM:MIX_W + (h + 1) * HEAD_DIM]
            v = qkv[r0:r0 + c, 2 * MIX_W + h * HEAD_DIM:2 * MIX_W + (h + 1) * HEAD_DIM]
            q = cq * lax.rsqrt(jnp.sum(cq * cq, axis=-1, keepdims=True) + EPS) * (HEAD_DIM ** -0.5)
            k = ck * lax.rsqrt(jnp.sum(ck * ck, axis=-1, keepdims=True) + EPS)
            beta = _lane_col(beta_all[r0:r0 + c, :], LANE_GB + h)
            gc = _lane_col(gc_all, LANE_GA + h)
            gc_row = gc_rows[LANE_GA + h:LANE_GA + h + 1, :]
            g_last = gc[c - 1:c, :]
            decay = jnp.exp(jnp.where(causal, gc - gc_row, -jnp.inf))
            kb = k.astype(BF16)
            kk = lax.dot_general(kb, kb, NT_DIMS, preferred_element_type=F32)
            a_strict = jnp.where(strict, beta * kk * decay, 0.0)
            t_inv = _unit_lower_inverse(a_strict, eye)
            rhs = jnp.concatenate([v * beta, k * (beta * jnp.exp(gc))], axis=-1)
            sol = jnp.dot(t_inv, rhs, precision=hp, preferred_element_type=F32)
            u = sol[:, :HEAD_DIM]
            w = sol[:, HEAD_DIM:]
            qk = lax.dot_general(q.astype(BF16), kb, NT_DIMS, preferred_element_type=F32)
            attn = jnp.where(causal, qk * decay, 0.0)
            k_tail = k * jnp.exp(g_last - gc)

            state = state_ref[h]
            sb = state.astype(BF16)
            v_new = u - jnp.dot(w.astype(BF16), sb, preferred_element_type=F32)
            vnb = v_new.astype(BF16)
            o = (jnp.dot((q * jnp.exp(gc)).astype(BF16), sb, preferred_element_type=F32)
                 + jnp.dot(attn.astype(BF16), vnb, preferred_element_type=F32))
            state_ref[h] = state * jnp.exp(g_last) + lax.dot_general(
                k_tail.astype(BF16), vnb, TN_DIMS, preferred_element_type=F32)

            z = yz_ref[r0:r0 + c, sl].astype(F32)
            o_ref[r0:r0 + c, sl] = (_rms(o, on_ref[...]) * _silu(z)).astype(BF16)


def _gdn(y, g, conv_w, alog_row, bias_row, onorm, batch, seq, tg):
    m = batch * seq
    nt = seq // tg
    rows = lambda b, i: b * nt + i
    ycol = lambda off: pl.BlockSpec((tg, MIX_W), lambda b, i: (rows(b, i), off // MIX_W))
    return pl.pallas_call(
        functools.partial(_gdn_kernel, tg=tg),
        grid=(batch, nt),
        in_specs=[
            ycol(COL_GQ), ycol(COL_GK), ycol(COL_GV), ycol(COL_GZ),
            pl.BlockSpec((tg, LANES), lambda b, i: (rows(b, i), 0)),
            _const_spec((GDN_CONV, 3 * MIX_W)),
            _const_spec((1, LANES)),
            _const_spec((1, LANES)),
            _const_spec((1, HEAD_DIM)),
        ],
        out_specs=pl.BlockSpec((tg, MIX_W), lambda b, i: (rows(b, i), 0)),
        out_shape=jax.ShapeDtypeStruct((m, MIX_W), BF16),
        scratch_shapes=[pltpu.VMEM((HEADS, HEAD_DIM, HEAD_DIM), F32),
                        pltpu.VMEM((tg + SUBLANES, 3 * MIX_W), F32)],
        compiler_params=_cparams("parallel", "arbitrary"),
        name="gdn",
    )(y, y, y, y, g, conv_w, alog_row, bias_row, onorm)


def _merge_kernel(x_ref, ya_ref, yb_ref, yc_ref, g0_ref, g1_ref, g2_ref, gb_ref,
                  woa_ref, wob_ref, woc_ref, wout_ref, o_ref, *, d):
    def branch(y_ref, w_ref, gate_ref, idx):
        gate = jax.nn.sigmoid(gate_ref[...].astype(F32) + gb_ref[:, idx * d:(idx + 1) * d])
        return gate * jnp.dot(y_ref[...], w_ref[...], preferred_element_type=F32)

    mixed = (branch(ya_ref, woa_ref, g0_ref, 0) + branch(yb_ref, wob_ref, g1_ref, 1)
             + branch(yc_ref, woc_ref, g2_ref, 2))
    o_ref[...] = x_ref[...] + jnp.dot(mixed.astype(BF16), wout_ref[...], preferred_element_type=F32)


def _merge(xf, ya, yb, yc, y, gate_bias, woa, wob, woc, wout, tm):
    m, d = xf.shape
    rowblk = lambda w, cb: pl.BlockSpec((tm, w), lambda i: (i, cb))
    return pl.pallas_call(
        functools.partial(_merge_kernel, d=d),
        grid=(m // tm,),
        in_specs=[
            rowblk(d, 0), rowblk(MIX_W, 0), rowblk(MIX_W, 0), rowblk(MIX_W, 0),
            rowblk(d, COL_GATES // d), rowblk(d, COL_GATES // d + 1), rowblk(d, COL_GATES // d + 2),
            _const_spec((1, 3 * d)),
            _const_spec((MIX_W, d)), _const_spec((MIX_W, d)), _const_spec((MIX_W, d)),
            _const_spec((d, d)),
        ],
        out_specs=rowblk(d, 0),
        out_shape=jax.ShapeDtypeStruct((m, d), F32),
        compiler_params=_cparams("parallel"),
        name="merge_out",
    )(xf, ya, yb, yc, y, y, y, gate_bias, woa, wob, woc, wout)


def _mem_kv_kernel(m_ref, g_ref, wkv_ref, kn_ref, k_ref, v_ref):
    mn = _rms(m_ref[...], g_ref[...]).astype(BF16)
    kv = jnp.dot(mn, wkv_ref[...], preferred_element_type=F32)
    for h in range(HEADS):
        sl = slice(h * HEAD_DIM, (h + 1) * HEAD_DIM)
        k_ref[:, sl] = _rms(kv[:, sl], kn_ref[...]).astype(BF16)
    v_ref[...] = kv[:, MIX_W:].astype(BF16)


def _mem_kv(memf, gain, wkv, knorm, batch, n_mem):
    d = memf.shape[1]
    return pl.pallas_call(
        _mem_kv_kernel,
        grid=(batch,),
        in_specs=[
            pl.BlockSpec((n_mem, d), lambda b: (b, 0)),
            _const_spec((1, d)),
            _const_spec((d, 2 * MIX_W)),
            _const_spec((1, HEAD_DIM)),
        ],
        out_specs=[pl.BlockSpec((n_mem, MIX_W), lambda b: (b, 0))] * 2,
        out_shape=[jax.ShapeDtypeStruct((batch * n_mem, MIX_W), BF16)] * 2,
        compiler_params=_cparams("parallel"),
        name="mem_kv",
    )(memf, gain, wkv, knorm)


def _mem_attn_kernel(x_ref, g_ref, wq_ref, qn_ref, k_ref, v_ref, wo_ref, o_ref):
    x = x_ref[...]
    hq = _rms(x, g_ref[...]).astype(BF16)
    q_all = jnp.dot(hq, wq_ref[...], preferred_element_type=F32)
    outs = []
    for h in range(HEADS):
        sl = slice(h * HEAD_DIM, (h + 1) * HEAD_DIM)
        q = (_rms(q_all[:, sl], qn_ref[...]) * (HEAD_DIM ** -0.5)).astype(BF16)
        s = lax.dot_general(q, k_ref[:, sl], NT_DIMS, preferred_element_type=F32)
        p = jnp.exp(s - jnp.max(s, axis=-1, keepdims=True))
        p = p / jnp.sum(p, axis=-1, keepdims=True)
        outs.append(jnp.dot(p.astype(BF16), v_ref[:, sl], preferred_element_type=F32).astype(BF16))
    o = jnp.concatenate(outs, axis=-1)
    o_ref[...] = x + jnp.dot(o, wo_ref[...], preferred_element_type=F32)


def _mem_attn(xf, gain, wq, qnorm, kmem, vmem, wo, batch, seq, n_mem, tm):
    m, d = xf.shape
    nt = seq // tm
    return pl.pallas_call(
        _mem_attn_kernel,
        grid=(batch, nt),
        in_specs=[
            pl.BlockSpec((tm, d), lambda b, i: (b * nt + i, 0)),
            _const_spec((1, d)),
            _const_spec((d, MIX_W)),
            _const_spec((1, HEAD_DIM)),
            pl.BlockSpec((n_mem, MIX_W), lambda b, i: (b, 0)),
            pl.BlockSpec((n_mem, MIX_W), lambda b, i: (b, 0)),
            _const_spec((MIX_W, d)),
        ],
        out_specs=pl.BlockSpec((tm, d), lambda b, i: (b * nt + i, 0)),
        out_shape=jax.ShapeDtypeStruct((m, d), F32),
        compiler_params=_cparams("parallel", "parallel"),
        name="mem_attn",
    )(xf, gain, wq, qnorm, kmem, vmem, wo)


def _ffn_kernel(x_ref, g_ref, wup_ref, cw_ref, cb_ref, wdn_ref, o_ref, carry_ref, xs_ref,
                *, tm, d_ff, n_chunk):
    @pl.when(pl.program_id(1) == 0)
    def _():
        carry_ref[...] = jnp.zeros_like(carry_ref)

    x = x_ref[...]
    h = _rms(x, g_ref[...]).astype(BF16)

    def conv_half(c0):
        u = jnp.dot(h, wup_ref[:, c0:c0 + n_chunk], preferred_element_type=F32)
        xs_ref[0:SUBLANES, :] = carry_ref[:, c0:c0 + n_chunk]
        xs_ref[SUBLANES:SUBLANES + tm, :] = u
        carry_ref[:, c0:c0 + n_chunk] = xs_ref[tm:tm + SUBLANES, :]
        out = jnp.broadcast_to(cb_ref[:, c0:c0 + n_chunk], (tm, n_chunk))
        for tap in range(FFN_CONV):
            off = SUBLANES - (FFN_CONV - 1) + tap
            out = out + cw_ref[tap:tap + 1, c0:c0 + n_chunk] * xs_ref[off:off + tm, :]
        return out

    acc = x
    for c0 in range(0, d_ff, n_chunk):
        a = conv_half(c0)
        b = conv_half(d_ff + c0)
        act = (_silu(a) * b).astype(BF16)
        acc = acc + jnp.dot(act, wdn_ref[c0:c0 + n_chunk, :], preferred_element_type=F32)
    o_ref[...] = acc


def _ffn(xf, gain, wup, conv_w, conv_b, wdn, batch, seq, tm, n_chunk):
    m, d = xf.shape
    d_ff = wdn.shape[0]
    nt = seq // tm
    return pl.pallas_call(
        functools.partial(_ffn_kernel, tm=tm, d_ff=d_ff, n_chunk=n_chunk),
        grid=(batch, nt),
        in_specs=[
            pl.BlockSpec((tm, d), lambda b, i: (b * nt + i, 0)),
            _const_spec((1, d)),
            _const_spec((d, 2 * d_ff)),
            _const_spec((FFN_CONV, 2 * d_ff)),
            _const_spec((1, 2 * d_ff)),
            _const_spec((d_ff, d)),
        ],
        out_specs=pl.BlockSpec((tm, d), lambda b, i: (b * nt + i, 0)),
        out_shape=jax.ShapeDtypeStruct((m, d), F32),
        scratch_shapes=[pltpu.VMEM((SUBLANES, 2 * d_ff), F32),
                        pltpu.VMEM((tm + SUBLANES, n_chunk), F32)],
        compiler_params=_cparams("parallel", "arbitrary"),
        name="conv_ffn",
    )(xf, gain, wup, conv_w, conv_b, wdn)


def _lane_row(values, lane0):
    return jnp.zeros((1, LANES), F32).at[0, lane0:lane0 + values.shape[0]].set(values.astype(F32))


def _pack_w_in(w):
    splits = (512, 512, 512, 4, 512, 512, 512, 4, 4, 512, 512, 512, 512)
    parts, off = [], 0
    for s in splits:
        parts.append(w[:, off:off + s])
        off += s
    gates = w[:, off:]
    fq, fk, fv, ff, gq, gk, gv, gb, ga, gz, sq, sk, sv = parts
    big = jnp.concatenate([fq, fk, fv, gq, gk, gv, gz, sq, sk, sv, gates], axis=1).astype(BF16)
    small = jnp.concatenate(
        [ff, gb, ga, jnp.zeros((w.shape[0], LANES - 3 * HEADS), w.dtype)], axis=1).astype(BF16)
    return big, small


def _tile(n, pref):
    return pref if n % pref == 0 else n


def kernel(x, mem, norm_mix, w_in, fox_fbias, fox_qnorm, fox_knorm, gdn_conv, gdn_a_log, gdn_dt_bias, gdn_onorm, gate_bias, w_oa, w_ob, w_oc, w_out, norm_xq, norm_mem, w_mq, w_mkv, mq_norm, mk_norm, w_mo, norm_ffn, w_up, ffn_conv, ffn_conv_b, w_down):
    batch, seq, d = x.shape
    n_mem = mem.shape[1]
    depth = w_in.shape[0]
    d_ff = w_down.shape[1]
    m = batch * seq
    assert d == 1024 and w_in.shape[2] == N_BIG + 3 * HEADS and seq % GDN_CHUNK == 0

    tm = _tile(seq, 512)
    tq = _tile(seq, 256)
    t_prep = _tile(seq, 256)
    tg = _tile(seq, 128)
    n_chunk = d_ff // 2

    xf = x.reshape(m, d)
    memf = mem.reshape(batch * n_mem, d)
    row = lambda v: v.reshape(1, -1).astype(F32)
    for l in range(depth):
        w_big, w_small = _pack_w_in(w_in[l])
        y, g = _inproj(xf, row(norm_mix[l]), w_big, w_small, tm)

        q_aug, k_aug = _fox_prep(y, g, _lane_row(fox_fbias[l], LANE_FF), row(fox_qnorm[l]),
                                 row(fox_knorm[l]), batch, seq, t_prep)
        ya = _fox_attn(q_aug, k_aug, y, batch, seq, tq)
        yb = _gdn(y, g, gdn_conv[l].astype(F32), _lane_row(gdn_a_log[l], LANE_GA),
                  _lane_row(gdn_dt_bias[l], LANE_GA), row(gdn_onorm[l]), batch, seq, tg)
        yc = _sb_attn(y, batch, seq, tq)

        xf = _merge(xf, ya, yb, yc, y, row(gate_bias[l]), w_oa[l].astype(BF16),
                    w_ob[l].astype(BF16), w_oc[l].astype(BF16), w_out[l].astype(BF16), tm)

        kmem, vmem = _mem_kv(memf, row(norm_mem[l]), w_mkv[l].astype(BF16), row(mk_norm[l]),
                             batch, n_mem)
        xf = _mem_attn(xf, row(norm_xq[l]), w_mq[l].astype(BF16), row(mq_norm[l]), kmem, vmem,
                       w_mo[l].astype(BF16), batch, seq, n_mem, tm)

        xf = _ffn(xf, row(norm_ffn[l]), w_up[l].astype(BF16), ffn_conv[l].astype(F32),
                  row(ffn_conv_b[l]), w_down[l].astype(BF16), batch, seq, tm, n_chunk)
    return xf.reshape(batch, seq, d)
```

```python
import functools

import jax
import jax.numpy as jnp
from jax import lax
from jax.experimental import pallas as pl
from jax.experimental.pallas import tpu as pltpu

F32 = jnp.float32
BF16 = jnp.bfloat16
EPS = 1e-6
LOG2E = 1.4426950408889634

HEADS = 4
HEAD_DIM = 128
MIX_W = HEADS * HEAD_DIM
GDN_CHUNK = 64
GDN_CONV = 4
FFN_CONV = 3
LANES = 128
SUBLANES = 8
VMEM_LIMIT = 56 * 1024 * 1024

COL_FQ, COL_FK, COL_FV = 0, 512, 1024
COL_GQ, COL_GK, COL_GV, COL_GZ = 1536, 2048, 2560, 3072
COL_SQ, COL_SK, COL_SV = 3584, 4096, 4608
COL_GATES = 5120
N_BIG = 8192
LANE_FF, LANE_GB, LANE_GA = 0, 4, 8

NT_DIMS = (((1,), (1,)), ((), ()))
TN_DIMS = (((0,), (0,)), ((), ()))


def _cparams(*sem):
    return pltpu.CompilerParams(dimension_semantics=sem, vmem_limit_bytes=VMEM_LIMIT)


def _const_spec(shape):
    nd = len(shape)
    return pl.BlockSpec(shape, lambda *_: (0,) * nd, pipeline_mode=pl.Buffered(1))


def _rms(xf, gain):
    return xf * lax.rsqrt(jnp.mean(xf * xf, axis=-1, keepdims=True) + EPS) * gain


def _softplus(x):
    return jnp.maximum(x, 0.0) + jnp.log1p(jnp.exp(-jnp.abs(x)))


def _silu(x):
    return x * jax.nn.sigmoid(x)


def _split3(x):
    a = x.astype(BF16)
    r = x - a.astype(F32)
    b = r.astype(BF16)
    c = (r - b.astype(F32)).astype(BF16)
    return a, b, c


def _lane_col(x, lane):
    idx = lax.broadcasted_iota(jnp.int32, x.shape, 1)
    return jnp.sum(jnp.where(idx == lane, x, 0.0), axis=-1, keepdims=True)


def _inproj_kernel(x_ref, g_ref, w_ref, ws_ref, y_ref, gs_ref, *, n_chunk):
    h = _rms(x_ref[...], g_ref[...]).astype(BF16)
    for c in range(0, N_BIG, n_chunk):
        y_ref[:, c:c + n_chunk] = jnp.dot(
            h, w_ref[:, c:c + n_chunk], preferred_element_type=F32).astype(BF16)
    gs_ref[...] = jnp.dot(h, ws_ref[...], preferred_element_type=F32)


def _inproj(xf, gain, w_big, w_small, tm):
    m, d = xf.shape
    return pl.pallas_call(
        functools.partial(_inproj_kernel, n_chunk=512),
        grid=(m // tm,),
        in_specs=[
            pl.BlockSpec((tm, d), lambda i: (i, 0)),
            _const_spec((1, d)),
            _const_spec((d, N_BIG)),
            _const_spec((d, LANES)),
        ],
        out_specs=[
            pl.BlockSpec((tm, N_BIG), lambda i: (i, 0)),
            pl.BlockSpec((tm, LANES), lambda i: (i, 0)),
        ],
        out_shape=[jax.ShapeDtypeStruct((m, N_BIG), BF16),
                   jax.ShapeDtypeStruct((m, LANES), F32)],
        compiler_params=_cparams("parallel"),
        name="inproj",
    )(xf, gain, w_big, w_small)


def _fox_prep_kernel(yq_ref, yk_ref, g_ref, fb_ref, qn_ref, kn_ref, qa_ref, ka_ref, carry_ref, *, t):
    @pl.when(pl.program_id(1) == 0)
    def _():
        carry_ref[...] = jnp.zeros_like(carry_ref)

    lane = lax.broadcasted_iota(jnp.int32, (t, LANES), 1)
    x = g_ref[...] + fb_ref[...]
    logf = jnp.minimum(x, 0.0) - jnp.log1p(jnp.exp(-jnp.abs(x)))
    logf = jnp.where(lane < LANE_FF + HEADS, logf, 0.0)
    row = lax.broadcasted_iota(jnp.int32, (t, t), 0)
    col = lax.broadcasted_iota(jnp.int32, (t, t), 1)
    tri = jnp.where(row >= col, 1.0, 0.0).astype(BF16)
    l1, l2, l3 = _split3(logf)
    c_all = (jnp.dot(tri, l1, preferred_element_type=F32)
             + jnp.dot(tri, l2, preferred_element_type=F32)
             + jnp.dot(tri, l3, preferred_element_type=F32)) + carry_ref[0:1, :]
    carry_ref[0:1, :] = c_all[t - 1:t, :]

    scale = HEAD_DIM ** -0.5 * LOG2E
    for h in range(HEADS):
        sl = slice(h * HEAD_DIM, (h + 1) * HEAD_DIM)
        qn = _rms(yq_ref[:, sl].astype(F32), qn_ref[...]) * scale
        kn = _rms(yk_ref[:, sl].astype(F32), kn_ref[...])
        cb = jnp.broadcast_to(_lane_col(c_all, LANE_FF + h) * LOG2E, (t, LANES))
        c1, c2, c3 = (p.astype(F32) for p in _split3(cb))
        aux_q = jnp.where(lane == 0, c1, jnp.where(lane == 1, c2, jnp.where(
            lane == 2, c3, jnp.where(lane < 6, 1.0, 0.0))))
        aux_k = jnp.where(lane < 3, 1.0, jnp.where(lane == 3, -c1, jnp.where(
            lane == 4, -c2, jnp.where(lane == 5, -c3, 0.0))))
        base = 2 * h * HEAD_DIM
        qa_ref[:, base:base + HEAD_DIM] = qn.astype(BF16)
        qa_ref[:, base + HEAD_DIM:base + 2 * HEAD_DIM] = aux_q.astype(BF16)
        ka_ref[:, base:base + HEAD_DIM] = kn.astype(BF16)
        ka_ref[:, base + HEAD_DIM:base + 2 * HEAD_DIM] = aux_k.astype(BF16)


def _fox_prep(y, g, fb_row, qn, kn, batch, seq, t):
    m = batch * seq
    nt = seq // t
    rows = lambda b, i: b * nt + i
    return pl.pallas_call(
        functools.partial(_fox_prep_kernel, t=t),
        grid=(batch, nt),
        in_specs=[
            pl.BlockSpec((t, MIX_W), lambda b, i: (rows(b, i), COL_FQ // MIX_W)),
            pl.BlockSpec((t, MIX_W), lambda b, i: (rows(b, i), COL_FK // MIX_W)),
            pl.BlockSpec((t, LANES), lambda b, i: (rows(b, i), 0)),
            _const_spec((1, LANES)),
            _const_spec((1, HEAD_DIM)),
            _const_spec((1, HEAD_DIM)),
        ],
        out_specs=[
            pl.BlockSpec((t, 2 * MIX_W), lambda b, i: (rows(b, i), 0)),
            pl.BlockSpec((t, 2 * MIX_W), lambda b, i: (rows(b, i), 0)),
        ],
        out_shape=[jax.ShapeDtypeStruct((m, 2 * MIX_W), BF16)] * 2,
        scratch_shapes=[pltpu.VMEM((SUBLANES, LANES), F32)],
        compiler_params=_cparams("parallel", "arbitrary"),
        name="fox_prep",
    )(y, y, g, fb_row, qn, kn)


def _causal_sweep(i, stage_a, stage_b):
    stage_a(i, 0, True)

    @pl.when(i == 0)
    def _():
        stage_b(i, 0, True)

    @pl.when(i > 0)
    def _():
        stage_a(i - 1, 1, False)
        stage_b(i, 0, True)
        n_pairs = (i - 1) // 2

        def body(p, carry):
            t = i - 1 - 2 * p
            stage_a(t - 1, 0, False)
            stage_b(t, 1, False)
            stage_a(t - 2, 1, False)
            stage_b(t - 1, 0, False)
            return carry

        lax.fori_loop(0, n_pairs, body, 0)
        t = i - 1 - 2 * n_pairs

        @pl.when(t == 0)
        def _():
            stage_b(0, 1, False)

        @pl.when(t == 1)
        def _():
            stage_a(0, 0, False)
            stage_b(1, 1, False)
            stage_b(0, 0, False)


def _fox_attn_kernel(q_ref, k_ref, v_ref, o_ref, s_ref, m_ref, l_ref, acc_ref, *, tq):
    m_ref[...] = jnp.full((tq, 1), -jnp.inf, F32)
    l_ref[...] = jnp.zeros((tq, 1), F32)
    acc_ref[...] = jnp.zeros((tq, HEAD_DIM), F32)

    def scores(j, slot, masked):
        del masked
        k = k_ref[pl.ds(pl.multiple_of(j * tq, tq), tq), :]
        s_ref[slot] = lax.dot_general(q_ref[...], k, NT_DIMS, preferred_element_type=F32)

    def softmax_pv(j, slot, masked):
        v = v_ref[pl.ds(pl.multiple_of(j * tq, tq), tq), :]
        s = s_ref[slot]
        if masked:
            row = lax.broadcasted_iota(jnp.int32, (tq, tq), 0)
            col = lax.broadcasted_iota(jnp.int32, (tq, tq), 1)
            s = jnp.where(col <= row, s, -jnp.inf)
        m_prev = m_ref[...]
        m_new = jnp.maximum(m_prev, jnp.max(s, axis=-1, keepdims=True))
        p = jnp.exp2(s - m_new)
        alpha = jnp.exp2(m_prev - m_new)
        l_ref[...] = alpha * l_ref[...] + jnp.sum(p, axis=-1, keepdims=True)
        acc_ref[...] = alpha * acc_ref[...] + jnp.dot(p.astype(BF16), v, preferred_element_type=F32)
        m_ref[...] = m_new

    _causal_sweep(pl.program_id(2), scores, softmax_pv)
    o_ref[...] = (acc_ref[...] / l_ref[...]).astype(BF16)


def _fox_attn(q_aug, k_aug, y, batch, seq, tq):
    m = batch * seq
    nq = seq // tq
    return pl.pallas_call(
        functools.partial(_fox_attn_kernel, tq=tq),
        grid=(batch, HEADS, nq),
        in_specs=[
            pl.BlockSpec((tq, 2 * HEAD_DIM), lambda b, h, i: (b * nq + i, h)),
            pl.BlockSpec((seq, 2 * HEAD_DIM), lambda b, h, i: (b, h)),
            pl.BlockSpec((seq, HEAD_DIM), lambda b, h, i: (b, COL_FV // HEAD_DIM + h)),
        ],
        out_specs=pl.BlockSpec((tq, HEAD_DIM), lambda b, h, i: (b * nq + i, h)),
        out_shape=jax.ShapeDtypeStruct((m, MIX_W), BF16),
        scratch_shapes=[pltpu.VMEM((2, tq, tq), F32),
                        pltpu.VMEM((tq, 1), F32), pltpu.VMEM((tq, 1), F32),
                        pltpu.VMEM((tq, HEAD_DIM), F32)],
        compiler_params=_cparams("parallel", "parallel", "arbitrary"),
        name="fox_attn",
    )(q_aug, k_aug, y)


def _sb_attn_kernel(q_ref, k_ref, v_ref, o_ref, zw_ref, tot_ref, after_ref, acc_ref, *, tq, tk):
    n_sub = tq // tk
    row_k = lax.broadcasted_iota(jnp.int32, (tk, tk), 0)
    col_k = lax.broadcasted_iota(jnp.int32, (tk, tk), 1)
    neg_suffix = jnp.where(row_k >= col_k, -1.0, 0.0).astype(BF16)
    after_ref[...] = jnp.zeros((tq, 1), F32)
    acc_ref[...] = jnp.zeros((tq, HEAD_DIM), F32)

    def valid_mask(sub):
        row = lax.broadcasted_iota(jnp.int32, (tq, tk), 0)
        col = lax.broadcasted_iota(jnp.int32, (tq, tk), 1) + sub * tk
        return col < row

    def logits(j, slot, masked):
        for sub in range(n_sub):
            k = k_ref[pl.ds(pl.multiple_of(j * tq + sub * tk, tk), tk), :]
            z = lax.dot_general(q_ref[...], k, NT_DIMS, preferred_element_type=F32)
            sp = jnp.maximum(z, 0.0) + jnp.log(1.0 + jnp.exp(-jnp.abs(z)))
            if masked:
                sp = jnp.where(valid_mask(sub), sp, 0.0)
            within = jnp.dot(sp.astype(BF16), neg_suffix, preferred_element_type=F32)
            zw_ref[slot, sub] = z + within
            tot_ref[slot, sub] = within[:, 0:1]

    def weights_pv(j, slot, masked):
        for sub in reversed(range(n_sub)):
            v = v_ref[pl.ds(pl.multiple_of(j * tq + sub * tk, tk), tk), :]
            after = after_ref[...]
            e = zw_ref[slot, sub] + after
            if masked:
                e = jnp.where(valid_mask(sub), e, -jnp.inf)
            a = jnp.exp(e)
            acc_ref[...] += jnp.dot(a.astype(BF16), v, preferred_element_type=F32)
            after_ref[...] = after + tot_ref[slot, sub]

    _causal_sweep(pl.program_id(2), logits, weights_pv)
    o_ref[...] = acc_ref[...].astype(BF16)


def _sb_attn(y, batch, seq, tq, tk):
    m = batch * seq
    nq = seq // tq
    return pl.pallas_call(
        functools.partial(_sb_attn_kernel, tq=tq, tk=tk),
        grid=(batch, HEADS, nq),
        in_specs=[
            pl.BlockSpec((tq, HEAD_DIM), lambda b, h, i: (b * nq + i, COL_SQ // HEAD_DIM + h)),
            pl.BlockSpec((seq, HEAD_DIM), lambda b, h, i: (b, COL_SK // HEAD_DIM + h)),
            pl.BlockSpec((seq, HEAD_DIM), lambda b, h, i: (b, COL_SV // HEAD_DIM + h)),
        ],
        out_specs=pl.BlockSpec((tq, HEAD_DIM), lambda b, h, i: (b * nq + i, h)),
        out_shape=jax.ShapeDtypeStruct((m, MIX_W), BF16),
        scratch_shapes=[pltpu.VMEM((2, tq // tk, tq, tk), F32),
                        pltpu.VMEM((2, tq // tk, tq, 1), F32),
                        pltpu.VMEM((tq, 1), F32), pltpu.VMEM((tq, HEAD_DIM), F32)],
        compiler_params=_cparams("parallel", "parallel", "arbitrary"),
        name="sb_attn",
    )(y, y, y)


def _bdot(a, b):
    return jnp.dot(a.astype(BF16), b.astype(BF16), preferred_element_type=F32)


def _unit_lower_inverses(a_list, eye):
    c = eye.shape[0]
    xs = [eye - a for a in a_list]
    ps = [_bdot(a, a) for a in a_list]
    power = 2
    while 2 * power < c:
        xps = [_bdot(jnp.concatenate([x, p], axis=0), p) for x, p in zip(xs, ps)]
        xs = [x + xp[:c] for x, xp in zip(xs, xps)]
        ps = [xp[c:] for xp in xps]
        power *= 2
    return [x + _bdot(x, p) for x, p in zip(xs, ps)]


def _gdn_kernel(yq_ref, yk_ref, yv_ref, yz_ref, g_ref, cw_ref, alog_ref, bias_ref, on_ref,
                o_ref, state_ref, xs_ref, *, tg):
    @pl.when(pl.program_id(1) == 0)
    def _():
        state_ref[...] = jnp.zeros_like(state_ref)
        xs_ref[0:SUBLANES, :] = jnp.zeros((SUBLANES, 3 * MIX_W), F32)

    c = GDN_CHUNK
    xs_ref[SUBLANES:SUBLANES + tg, 0:MIX_W] = yq_ref[...].astype(F32)
    xs_ref[SUBLANES:SUBLANES + tg, MIX_W:2 * MIX_W] = yk_ref[...].astype(F32)
    xs_ref[SUBLANES:SUBLANES + tg, 2 * MIX_W:3 * MIX_W] = yv_ref[...].astype(F32)
    conv = jnp.zeros((tg, 3 * MIX_W), F32)
    for tap in range(GDN_CONV):
        off = SUBLANES - (GDN_CONV - 1) + tap
        conv = conv + cw_ref[tap:tap + 1, :] * xs_ref[off:off + tg, :]
    xs_ref[0:SUBLANES, :] = xs_ref[tg:tg + SUBLANES, :]
    qkv = _silu(conv)

    g_all = g_ref[...]
    beta_all = jax.nn.sigmoid(g_all)
    glog_all = -jnp.exp(alog_ref[...]) * _softplus(g_all + bias_ref[...])
    lane = lax.broadcasted_iota(jnp.int32, (tg, LANES), 1)
    glog_all = jnp.where((lane >= LANE_GA) & (lane < LANE_GA + HEADS), glog_all, 0.0)

    row = lax.broadcasted_iota(jnp.int32, (c, c), 0)
    col = lax.broadcasted_iota(jnp.int32, (c, c), 1)
    causal = row >= col
    strict = row > col
    eye = jnp.where(row == col, 1.0, 0.0)
    tril = jnp.where(causal, 1.0, 0.0).astype(BF16)

    items = [(n, h) for n in range(tg // c) for h in range(HEADS)]
    gc_alls, gc_rows = [], []
    for n in range(tg // c):
        g1, g2, g3 = _split3(glog_all[n * c:(n + 1) * c, :])
        gc_all = (jnp.dot(tril, g1, preferred_element_type=F32)
                  + jnp.dot(tril, g2, preferred_element_type=F32)
                  + jnp.dot(tril, g3, preferred_element_type=F32))
        gc_alls.append(gc_all)
        gc_rows.append(gc_all.T)

    pre = []
    for n, h in items:
        r0 = n * c
        cq = qkv[r0:r0 + c, h * HEAD_DIM:(h + 1) * HEAD_DIM]
        ck = qkv[r0:r0 + c, MIX_W + h * HEAD_DIM:MIX_W + (h + 1) * HEAD_DIM]
        v = qkv[r0:r0 + c, 2 * MIX_W + h * HEAD_DIM:2 * MIX_W + (h + 1) * HEAD_DIM]
        q = cq * lax.rsqrt(jnp.sum(cq * cq, axis=-1, keepdims=True) + EPS) * (HEAD_DIM ** -0.5)
        k = ck * lax.rsqrt(jnp.sum(ck * ck, axis=-1, keepdims=True) + EPS)
        beta = _lane_col(beta_all[r0:r0 + c, :], LANE_GB + h)
        gc = _lane_col(gc_alls[n], LANE_GA + h)
        gc_row = gc_rows[n][LANE_GA + h:LANE_GA + h + 1, :]
        g_last = gc[c - 1:c, :]
        decay = jnp.exp(jnp.where(causal, gc - gc_row, -jnp.inf))
        kb = k.astype(BF16)
        kk = lax.dot_general(kb, kb, NT_DIMS, preferred_element_type=F32)
        qk = lax.dot_general(q.astype(BF16), kb, NT_DIMS, preferred_element_type=F32)
        pre.append(dict(
            a_strict=jnp.where(strict, beta * kk * decay, 0.0),
            attn=jnp.where(causal, qk * decay, 0.0).astype(BF16),
            rhs=jnp.concatenate([v * beta, k * (beta * jnp.exp(gc))], axis=-1),
            q_dec=(q * jnp.exp(gc)).astype(BF16),
            k_tail=(k * jnp.exp(g_last - gc)).astype(BF16),
            s_dec=jnp.exp(g_last)))

    t_invs = _unit_lower_inverses([p["a_strict"] for p in pre], eye)
    sols = [_bdot(t, p["rhs"]) for t, p in zip(t_invs, pre)]

    for n in range(tg // c):
        r0 = n * c
        idx = [n * HEADS + h for h in range(HEADS)]
        states = [state_ref[h] for h in range(HEADS)]
        sbs = [s.astype(BF16) for s in states]
        vnbs = [(sols[i][:, :HEAD_DIM] - jnp.dot(sols[i][:, HEAD_DIM:].astype(BF16), sb,
                                                 preferred_element_type=F32)).astype(BF16)
                for i, sb in zip(idx, sbs)]
        outs = [jnp.dot(pre[i]["q_dec"], sb, preferred_element_type=F32)
                + jnp.dot(pre[i]["attn"], vnb, preferred_element_type=F32)
                for i, sb, vnb in zip(idx, sbs, vnbs)]
        for h in range(HEADS):
            i = idx[h]
            state_ref[h] = states[h] * pre[i]["s_dec"] + lax.dot_general(
                pre[i]["k_tail"], vnbs[h], TN_DIMS, preferred_element_type=F32)
            sl = slice(h * HEAD_DIM, (h + 1) * HEAD_DIM)
            z = yz_ref[r0:r0 + c, sl].astype(F32)
            o_ref[r0:r0 + c, sl] = (_rms(outs[h], on_ref[...]) * _silu(z)).astype(BF16)


def _gdn(y, g, conv_w, alog_row, bias_row, onorm, batch, seq, tg):
    m = batch * seq
    nt = seq // tg
    rows = lambda b, i: b * nt + i
    ycol = lambda off: pl.BlockSpec((tg, MIX_W), lambda b, i: (rows(b, i), off // MIX_W))
    return pl.pallas_call(
        functools.partial(_gdn_kernel, tg=tg),
        grid=(batch, nt),
        in_specs=[
            ycol(COL_GQ), ycol(COL_GK), ycol(COL_GV), ycol(COL_GZ),
            pl.BlockSpec((tg, LANES), lambda b, i: (rows(b, i), 0)),
            _const_spec((GDN_CONV, 3 * MIX_W)),
            _const_spec((1, LANES)),
            _const_spec((1, LANES)),
            _const_spec((1, HEAD_DIM)),
        ],
        out_specs=pl.BlockSpec((tg, MIX_W), lambda b, i: (rows(b, i), 0)),
        out_shape=jax.ShapeDtypeStruct((m, MIX_W), BF16),
        scratch_shapes=[pltpu.VMEM((HEADS, HEAD_DIM, HEAD_DIM), F32),
                        pltpu.VMEM((tg + SUBLANES, 3 * MIX_W), F32)],
        compiler_params=_cparams("parallel", "arbitrary"),
        name="gdn",
    )(y, y, y, y, g, conv_w, alog_row, bias_row, onorm)


def _merge_kernel(x_ref, ya_ref, yb_ref, yc_ref, g0_ref, g1_ref, g2_ref, gb_ref,
                  woa_ref, wob_ref, woc_ref, wout_ref, o_ref, *, d):
    def branch(y_ref, w_ref, gate_ref, idx):
        gate = jax.nn.sigmoid(gate_ref[...].astype(F32) + gb_ref[:, idx * d:(idx + 1) * d])
        return gate * jnp.dot(y_ref[...], w_ref[...], preferred_element_type=F32)

    mixed = (branch(ya_ref, woa_ref, g0_ref, 0) + branch(yb_ref, wob_ref, g1_ref, 1)
             + branch(yc_ref, woc_ref, g2_ref, 2))
    o_ref[...] = x_ref[...] + jnp.dot(mixed.astype(BF16), wout_ref[...], preferred_element_type=F32)


def _merge(xf, ya, yb, yc, y, gate_bias, woa, wob, woc, wout, tm):
    m, d = xf.shape
    rowblk = lambda w, cb: pl.BlockSpec((tm, w), lambda i: (i, cb))
    return pl.pallas_call(
        functools.partial(_merge_kernel, d=d),
        grid=(m // tm,),
        in_specs=[
            rowblk(d, 0), rowblk(MIX_W, 0), rowblk(MIX_W, 0), rowblk(MIX_W, 0),
            rowblk(d, COL_GATES // d), rowblk(d, COL_GATES // d + 1), rowblk(d, COL_GATES // d + 2),
            _const_spec((1, 3 * d)),
            _const_spec((MIX_W, d)), _const_spec((MIX_W, d)), _const_spec((MIX_W, d)),
            _const_spec((d, d)),
        ],
        out_specs=rowblk(d, 0),
        out_shape=jax.ShapeDtypeStruct((m, d), F32),
        compiler_params=_cparams("parallel"),
        name="merge_out",
    )(xf, ya, yb, yc, y, y, y, gate_bias, woa, wob, woc, wout)


def _mem_kv_kernel(m_ref, g_ref, wkv_ref, kn_ref, k_ref, v_ref):
    mn = _rms(m_ref[...], g_ref[...]).astype(BF16)
    kv = jnp.dot(mn, wkv_ref[...], preferred_element_type=F32)
    for h in range(HEADS):
        sl = slice(h * HEAD_DIM, (h + 1) * HEAD_DIM)
        k_ref[:, sl] = _rms(kv[:, sl], kn_ref[...]).astype(BF16)
    v_ref[...] = kv[:, MIX_W:].astype(BF16)


def _mem_kv(memf, gain, wkv, knorm, batch, n_mem):
    d = memf.shape[1]
    return pl.pallas_call(
        _mem_kv_kernel,
        grid=(batch,),
        in_specs=[
            pl.BlockSpec((n_mem, d), lambda b: (b, 0)),
            _const_spec((1, d)),
            _const_spec((d, 2 * MIX_W)),
            _const_spec((1, HEAD_DIM)),
        ],
        out_specs=[pl.BlockSpec((n_mem, MIX_W), lambda b: (b, 0))] * 2,
        out_shape=[jax.ShapeDtypeStruct((batch * n_mem, MIX_W), BF16)] * 2,
        compiler_params=_cparams("parallel"),
        name="mem_kv",
    )(memf, gain, wkv, knorm)


def _mem_attn_kernel(x_ref, g_ref, wq_ref, qn_ref, k_ref, v_ref, wo_ref, o_ref):
    x = x_ref[...]
    hq = _rms(x, g_ref[...]).astype(BF16)
    q_all = jnp.dot(hq, wq_ref[...], preferred_element_type=F32)
    outs = []
    for h in range(HEADS):
        sl = slice(h * HEAD_DIM, (h + 1) * HEAD_DIM)
        q = (_rms(q_all[:, sl], qn_ref[...]) * (HEAD_DIM ** -0.5)).astype(BF16)
        s = lax.dot_general(q, k_ref[:, sl], NT_DIMS, preferred_element_type=F32)
        p = jnp.exp(s - jnp.max(s, axis=-1, keepdims=True))
        p = p / jnp.sum(p, axis=-1, keepdims=True)
        outs.append(jnp.dot(p.astype(BF16), v_ref[:, sl], preferred_element_type=F32).astype(BF16))
    o = jnp.concatenate(outs, axis=-1)
    o_ref[...] = x + jnp.dot(o, wo_ref[...], preferred_element_type=F32)


def _mem_attn(xf, gain, wq, qnorm, kmem, vmem, wo, batch, seq, n_mem, tm):
    m, d = xf.shape
    nt = seq // tm
    return pl.pallas_call(
        _mem_attn_kernel,
        grid=(batch, nt),
        in_specs=[
            pl.BlockSpec((tm, d), lambda b, i: (b * nt + i, 0)),
            _const_spec((1, d)),
            _const_spec((d, MIX_W)),
            _const_spec((1, HEAD_DIM)),
            pl.BlockSpec((n_mem, MIX_W), lambda b, i: (b, 0)),
            pl.BlockSpec((n_mem, MIX_W), lambda b, i: (b, 0)),
            _const_spec((MIX_W, d)),
        ],
        out_specs=pl.BlockSpec((tm, d), lambda b, i: (b * nt + i, 0)),
        out_shape=jax.ShapeDtypeStruct((m, d), F32),
        compiler_params=_cparams("parallel", "parallel"),
        name="mem_attn",
    )(xf, gain, wq, qnorm, kmem, vmem, wo)


def _ffn_kernel(x_ref, g_ref, wup_ref, cw_ref, cb_ref, wdn_ref, o_ref, carry_ref, xs_ref,
                *, tm, d_ff, n_chunk):
    @pl.when(pl.program_id(1) == 0)
    def _():
        carry_ref[...] = jnp.zeros_like(carry_ref)

    x = x_ref[...]
    h = _rms(x, g_ref[...]).astype(BF16)

    def conv_half(c0):
        u = jnp.dot(h, wup_ref[:, c0:c0 + n_chunk], preferred_element_type=F32)
        xs_ref[0:SUBLANES, :] = carry_ref[:, c0:c0 + n_chunk]
        xs_ref[SUBLANES:SUBLANES + tm, :] = u
        carry_ref[:, c0:c0 + n_chunk] = xs_ref[tm:tm + SUBLANES, :]
        out = jnp.broadcast_to(cb_ref[:, c0:c0 + n_chunk], (tm, n_chunk))
        for tap in range(FFN_CONV):
            off = SUBLANES - (FFN_CONV - 1) + tap
            out = out + cw_ref[tap:tap + 1, c0:c0 + n_chunk] * xs_ref[off:off + tm, :]
        return out

    acc = x
    for c0 in range(0, d_ff, n_chunk):
        a = conv_half(c0)
        b = conv_half(d_ff + c0)
        act = (_silu(a) * b).astype(BF16)
        acc = acc + jnp.dot(act, wdn_ref[c0:c0 + n_chunk, :], preferred_element_type=F32)
    o_ref[...] = acc


def _ffn(xf, gain, wup, conv_w, conv_b, wdn, batch, seq, tm, n_chunk):
    m, d = xf.shape
    d_ff = wdn.shape[0]
    nt = seq // tm
    return pl.pallas_call(
        functools.partial(_ffn_kernel, tm=tm, d_ff=d_ff, n_chunk=n_chunk),
        grid=(batch, nt),
        in_specs=[
            pl.BlockSpec((tm, d), lambda b, i: (b * nt + i, 0)),
            _const_spec((1, d)),
            _const_spec((d, 2 * d_ff)),
            _const_spec((FFN_CONV, 2 * d_ff)),
            _const_spec((1, 2 * d_ff)),
            _const_spec((d_ff, d)),
        ],
        out_specs=pl.BlockSpec((tm, d), lambda b, i: (b * nt + i, 0)),
        out_shape=jax.ShapeDtypeStruct((m, d), F32),
        scratch_shapes=[pltpu.VMEM((SUBLANES, 2 * d_ff), F32),
                        pltpu.VMEM((tm + SUBLANES, n_chunk), F32)],
        compiler_params=_cparams("parallel", "arbitrary"),
        name="conv_ffn",
    )(xf, gain, wup, conv_w, conv_b, wdn)


def _lane_row(values, lane0):
    return jnp.zeros((1, LANES), F32).at[0, lane0:lane0 + values.shape[0]].set(values.astype(F32))


def _pack_w_in(w):
    splits = (512, 512, 512, 4, 512, 512, 512, 4, 4, 512, 512, 512, 512)
    parts, off = [], 0
    for s in splits:
        parts.append(w[:, off:off + s])
        off += s
    gates = w[:, off:]
    fq, fk, fv, ff, gq, gk, gv, gb, ga, gz, sq, sk, sv = parts
    sq = sq * (HEAD_DIM ** -0.5)
    big =jnp.concatenate([fq, fk, fv, gq, gk, gv, gz, sq, sk, sv, gates], axis=1).astype(BF16)
    small = jnp.concatenate(
        [ff, gb, ga, jnp.zeros((w.shape[0], LANES - 3 * HEADS), w.dtype)], axis=1).astype(BF16)
    return big, small


def _tile(n, pref):
    return pref if n % pref == 0 else n


def kernel(x, mem, norm_mix, w_in, fox_fbias, fox_qnorm, fox_knorm, gdn_conv, gdn_a_log, gdn_dt_bias, gdn_onorm, gate_bias, w_oa, w_ob, w_oc, w_out, norm_xq, norm_mem, w_mq, w_mkv, mq_norm, mk_norm, w_mo, norm_ffn, w_up, ffn_conv, ffn_conv_b, w_down):
    batch, seq, d = x.shape
    n_mem = mem.shape[1]
    depth = w_in.shape[0]
    d_ff = w_down.shape[1]
    m = batch * seq
    assert d == 1024 and w_in.shape[2] == N_BIG + 3 * HEADS and seq % GDN_CHUNK == 0

    tm = _tile(seq, 512)
    tq = _tile(seq, 512)
    tk_sb = _tile(tq, 256)
    t_prep = _tile(seq, 256)
    tg = _tile(seq, 128)
    n_chunk = d_ff // 2

    xf = x.reshape(m, d)
    memf = mem.reshape(batch * n_mem, d)
    row = lambda v: v.reshape(1, -1).astype(F32)
    for l in range(depth):
        w_big, w_small = _pack_w_in(w_in[l])
        y, g = _inproj(xf, row(norm_mix[l]), w_big, w_small, tm)

        q_aug, k_aug = _fox_prep(y, g, _lane_row(fox_fbias[l], LANE_FF), row(fox_qnorm[l]),
                                 row(fox_knorm[l]), batch, seq, t_prep)
        ya = _fox_attn(q_aug, k_aug, y, batch, seq, tq)
        yb = _gdn(y, g, gdn_conv[l].astype(F32), _lane_row(gdn_a_log[l], LANE_GA),
                  _lane_row(gdn_dt_bias[l], LANE_GA), row(gdn_onorm[l]), batch, seq, tg)
        yc = _sb_attn(y, batch, seq, tq, tk_sb)

        xf = _merge(xf, ya, yb, yc, y, row(gate_bias[l]), w_oa[l].astype(BF16),
                    w_ob[l].astype(BF16), w_oc[l].astype(BF16), w_out[l].astype(BF16), tm)

        kmem, vmem = _mem_kv(memf, row(norm_mem[l]), w_mkv[l].astype(BF16), row(mk_norm[l]),
                             batch, n_mem)
        xf = _mem_attn(xf, row(norm_xq[l]), w_mq[l].astype(BF16), row(mq_norm[l]), kmem, vmem,
                       w_mo[l].astype(BF16), batch, seq, n_mem, tm)

        xf = _ffn(xf, row(norm_ffn[l]), w_up[l].astype(BF16), ffn_conv[l].astype(F32),
                  row(ffn_conv_b[l]), w_down[l].astype(BF16), batch, seq, tm, n_chunk)
    return xf.reshape(batch, seq, d)
```

```python
import functools

import jax
import jax.numpy as jnp
from jax import lax
from jax.experimental import pallas as pl
from jax.experimental.pallas import tpu as pltpu

F32 = jnp.float32
BF16 = jnp.bfloat16
EPS = 1e-6
LOG2E = 1.4426950408889634
SKIP_EXP = -110.0
SKIP_EXP2 = -155.0
NORM_SLACK = 1.01

HEADS = 4
HEAD_DIM = 128
MIX_W = HEADS * HEAD_DIM
GDN_CHUNK = 64
GDN_CONV = 4
FFN_CONV = 3
LANES = 128
SUBLANES = 8
VMEM_LIMIT = 56 * 1024 * 1024

COL_FQ, COL_FK, COL_FV = 0, 512, 1024
COL_GQ, COL_GK, COL_GV, COL_GZ = 1536, 2048, 2560, 3072
COL_SQ, COL_SK, COL_SV = 3584, 4096, 4608
COL_GATES = 5120
N_BIG = 8192
LANE_FF, LANE_GB, LANE_GA = 0, 4, 8

NT_DIMS = (((1,), (1,)), ((), ()))
TN_DIMS = (((0,), (0,)), ((), ()))


def _cparams(*sem):
    return pltpu.CompilerParams(dimension_semantics=sem, vmem_limit_bytes=VMEM_LIMIT)


def _const_spec(shape):
    nd = len(shape)
    return pl.BlockSpec(shape, lambda *_: (0,) * nd, pipeline_mode=pl.Buffered(1))


def _rms(xf, gain):
    return xf * lax.rsqrt(jnp.mean(xf * xf, axis=-1, keepdims=True) + EPS) * gain


def _softplus(x):
    return jnp.maximum(x, 0.0) + jnp.log1p(jnp.exp(-jnp.abs(x)))


def _silu(x):
    return x * jax.nn.sigmoid(x)


def _split3(x):
    a = x.astype(BF16)
    r = x - a.astype(F32)
    b = r.astype(BF16)
    c = (r - b.astype(F32)).astype(BF16)
    return a, b, c


def _lane_col(x, lane):
    idx = lax.broadcasted_iota(jnp.int32, x.shape, 1)
    return jnp.sum(jnp.where(idx == lane, x, 0.0), axis=-1, keepdims=True)


def _inproj_kernel(x_ref, g_ref, w_ref, ws_ref, y_ref, gs_ref, *, n_chunk):
    h = _rms(x_ref[...], g_ref[...]).astype(BF16)
    for c in range(0, N_BIG, n_chunk):
        y_ref[:, c:c + n_chunk] = jnp.dot(
            h, w_ref[:, c:c + n_chunk], preferred_element_type=F32).astype(BF16)
    gs_ref[...] = jnp.dot(h, ws_ref[...], preferred_element_type=F32)


def _inproj(xf, gain, w_big, w_small, tm):
    m, d = xf.shape
    return pl.pallas_call(
        functools.partial(_inproj_kernel, n_chunk=512),
        grid=(m // tm,),
        in_specs=[
            pl.BlockSpec((tm, d), lambda i: (i, 0)),
            _const_spec((1, d)),
            _const_spec((d, N_BIG)),
            _const_spec((d, LANES)),
        ],
        out_specs=[
            pl.BlockSpec((tm, N_BIG), lambda i: (i, 0)),
            pl.BlockSpec((tm, LANES), lambda i: (i, 0)),
        ],
        out_shape=[jax.ShapeDtypeStruct((m, N_BIG), BF16),
                   jax.ShapeDtypeStruct((m, LANES), F32)],
        compiler_params=_cparams("parallel"),
        name="inproj",
    )(xf, gain, w_big, w_small)


def _fox_prep_kernel(yq_ref, yk_ref, g_ref, fb_ref, qn_ref, kn_ref, qa_ref, ka_ref, ce_ref,
                     carry_ref, *, t):
    @pl.when(pl.program_id(1) == 0)
    def _():
        carry_ref[...] = jnp.zeros_like(carry_ref)

    lane = lax.broadcasted_iota(jnp.int32, (t, LANES), 1)
    x = g_ref[...] + fb_ref[...]
    logf = jnp.minimum(x, 0.0) - jnp.log1p(jnp.exp(-jnp.abs(x)))
    logf = jnp.where(lane < LANE_FF + HEADS, logf, 0.0)
    row = lax.broadcasted_iota(jnp.int32, (t, t), 0)
    col = lax.broadcasted_iota(jnp.int32, (t, t), 1)
    tri = jnp.where(row >= col, 1.0, 0.0).astype(BF16)
    l1, l2, l3 = _split3(logf)
    c_all = (jnp.dot(tri, l1, preferred_element_type=F32)
             + jnp.dot(tri, l2, preferred_element_type=F32)
             + jnp.dot(tri, l3, preferred_element_type=F32)) + carry_ref[0:1, :]
    carry_ref[0:1, :] = c_all[t - 1:t, :]
    ce_ref[0] = jnp.broadcast_to(c_all[t - 1:t, :] * LOG2E, (SUBLANES, LANES))

    scale = HEAD_DIM ** -0.5 * LOG2E
    for h in range(HEADS):
        sl = slice(h * HEAD_DIM, (h + 1) * HEAD_DIM)
        qn = _rms(yq_ref[:, sl].astype(F32), qn_ref[...]) * scale
        kn = _rms(yk_ref[:, sl].astype(F32), kn_ref[...])
        cb = jnp.broadcast_to(_lane_col(c_all, LANE_FF + h) * LOG2E, (t, LANES))
        c1, c2, c3 = (p.astype(F32) for p in _split3(cb))
        aux_q = jnp.where(lane == 0, c1, jnp.where(lane == 1, c2, jnp.where(
            lane == 2, c3, jnp.where(lane < 6, 1.0, 0.0))))
        aux_k = jnp.where(lane < 3, 1.0, jnp.where(lane == 3, -c1, jnp.where(
            lane == 4, -c2, jnp.where(lane == 5, -c3, 0.0))))
        base = 2 * h * HEAD_DIM
        qa_ref[:, base:base + HEAD_DIM] = qn.astype(BF16)
        qa_ref[:, base + HEAD_DIM:base + 2 * HEAD_DIM] = aux_q.astype(BF16)
        ka_ref[:, base:base + HEAD_DIM] = kn.astype(BF16)
        ka_ref[:, base + HEAD_DIM:base + 2 * HEAD_DIM] = aux_k.astype(BF16)


def _fox_prep(y, g, fb_row, qn, kn, batch, seq, t):
    m = batch * seq
    nt = seq // t
    rows = lambda b, i: b * nt + i
    return pl.pallas_call(
        functools.partial(_fox_prep_kernel, t=t),
        grid=(batch, nt),
        in_specs=[
            pl.BlockSpec((t, MIX_W), lambda b, i: (rows(b, i), COL_FQ // MIX_W)),
            pl.BlockSpec((t, MIX_W), lambda b, i: (rows(b, i), COL_FK // MIX_W)),
            pl.BlockSpec((t, LANES), lambda b, i: (rows(b, i), 0)),
            _const_spec((1, LANES)),
            _const_spec((1, HEAD_DIM)),
            _const_spec((1, HEAD_DIM)),
        ],
        out_specs=[
            pl.BlockSpec((t, 2 * MIX_W), lambda b, i: (rows(b, i), 0)),
            pl.BlockSpec((t, 2 * MIX_W), lambda b, i: (rows(b, i), 0)),
            pl.BlockSpec((1, SUBLANES, LANES), lambda b, i: (rows(b, i), 0, 0)),
        ],
        out_shape=[jax.ShapeDtypeStruct((m, 2 * MIX_W), BF16)] * 2
        + [jax.ShapeDtypeStruct((batch * nt, SUBLANES, LANES), F32)],
        scratch_shapes=[pltpu.VMEM((SUBLANES, LANES), F32)],
        compiler_params=_cparams("parallel", "arbitrary"),
        name="fox_prep",
    )(y, y, g, fb_row, qn, kn)


def _bounded_sweep(i, stage_a, stage_b, first_tile_fn):
    stage_a(i, 0, True)

    @pl.when(i == 0)
    def _():
        stage_b(i, 0, True)

    @pl.when(i > 0)
    def _():
        stage_a(i - 1, 1, False)
        stage_b(i, 0, True)
        lo = first_tile_fn()
        n = i - lo
        n_pairs = (n - 1) // 2

        def body(p, carry):
            t = i - 1 - 2 * p
            stage_a(t - 1, 0, False)
            stage_b(t, 1, False)
            stage_a(t - 2, 1, False)
            stage_b(t - 1, 0, False)
            return carry

        lax.fori_loop(0, n_pairs, body, 0)
        t = i - 1 - 2 * n_pairs

        @pl.when(jnp.logical_and(n >= 1, t == lo))
        def _():
            stage_b(t, 1, False)

        @pl.when(jnp.logical_and(n >= 1, t == lo + 1))
        def _():
            stage_a(lo, 0, False)
            stage_b(t, 1, False)
            stage_b(lo, 0, False)


def _checked_sweep(i, stage_a, stage_b, done_fn):
    stage_a(i, 0, True)

    @pl.when(i == 0)
    def _():
        stage_b(i, 0, True)

    @pl.when(i > 0)
    def _():
        stage_a(i - 1, 1, False)
        stage_b(i, 0, True)

        def body(carry):
            t = carry[0]
            stage_a(jnp.maximum(t - 1, 0), 0, False)
            stage_b(t, 1, False)
            done_1 = done_fn()

            @pl.when(jnp.logical_and(t >= 1, jnp.logical_not(done_1)))
            def _():
                stage_a(jnp.maximum(t - 2, 0), 1, False)
                stage_b(t - 1, 0, False)

            return t - 2, jnp.logical_or(t < 2, jnp.logical_or(done_1, done_fn()))

        lax.while_loop(lambda carry: jnp.logical_not(carry[1]), body, (i - 1, done_fn()))


def _max_sq_norm(ref, tq, seq, width):
    def chunk(c, best):
        x = ref[pl.ds(pl.multiple_of(c * tq, tq), tq), 0:width].astype(F32)
        return jnp.maximum(best, jnp.sum(x * x, axis=-1, keepdims=True))

    best = lax.fori_loop(0, seq // tq, chunk, jnp.zeros((tq, 1), F32))
    return jnp.max(best, axis=0, keepdims=True)


def _fox_attn_kernel(cend_ref, q_ref, k_ref, v_ref, o_ref, s_ref, m_ref, l_ref, acc_ref, zb_ref,
                     kmax_ref, *, tq, seq):
    @pl.when(pl.program_id(2) == 0)
    def _():
        kmax_ref[...] = jnp.broadcast_to(_max_sq_norm(k_ref, tq, seq, HEAD_DIM), kmax_ref.shape)

    m_ref[...] = jnp.full((tq, LANES), -jnp.inf, F32)
    l_ref[...] = jnp.zeros((tq, LANES), F32)
    acc_ref[...] = jnp.zeros((tq, HEAD_DIM), F32)

    qf = q_ref[:, 0:HEAD_DIM].astype(F32)
    aux = q_ref[:, HEAD_DIM:2 * HEAD_DIM].astype(F32)
    lane = lax.broadcasted_iota(jnp.int32, (tq, LANES), 1)
    c_q = jnp.sum(jnp.where(lane < 3, aux, 0.0), axis=-1, keepdims=True)
    q_sq = jnp.sum(qf * qf, axis=-1, keepdims=True)
    zb_ref[...] = jnp.sqrt(q_sq * kmax_ref[0:1, 0:1]) * NORM_SLACK + c_q

    def scores(j, slot, masked):
        del masked
        k = k_ref[pl.ds(pl.multiple_of(j * tq, tq), tq), :]
        s_ref[slot] = lax.dot_general(q_ref[...], k, NT_DIMS, preferred_element_type=F32)

    def softmax_pv(j, slot, masked):
        v = v_ref[pl.ds(pl.multiple_of(j * tq, tq), tq), :]
        s = s_ref[slot]
        if masked:
            row = lax.broadcasted_iota(jnp.int32, (tq, tq), 0)
            col = lax.broadcasted_iota(jnp.int32, (tq, tq), 1)
            s = jnp.where(col <= row, s, -jnp.inf)
        m_prev = m_ref[...]
        m_new = jnp.maximum(m_prev, jnp.max(s, axis=-1, keepdims=True))
        p = jnp.exp2(s - jnp.concatenate([m_new] * (tq // LANES), axis=1))
        alpha = jnp.exp2(m_prev - m_new)
        l_ref[...] = alpha * l_ref[...] + jnp.sum(p, axis=-1, keepdims=True)
        acc_ref[...] = alpha * acc_ref[...] + jnp.dot(p.astype(BF16), v, preferred_element_type=F32)
        m_ref[...] = m_new

    def first_tile_fn():
        slack = jnp.max(zb_ref[...] - m_ref[...])
        row = pl.program_id(0) * HEADS + pl.program_id(1)

        def count(j, lo):
            return lo + jnp.where(slack - cend_ref[row, j] < SKIP_EXP2, 1, 0)

        return lax.fori_loop(0, pl.program_id(2), count, 0)

    _bounded_sweep(pl.program_id(2), scores, softmax_pv, first_tile_fn)
    o_ref[...] = (acc_ref[...] / l_ref[...]).astype(BF16)


def _fox_attn(cend, q_aug, k_aug, y, batch, seq, tq):
    m = batch * seq
    nq = seq // tq
    return pl.pallas_call(
        functools.partial(_fox_attn_kernel, tq=tq, seq=seq),
        grid=(batch, HEADS, nq),
        in_specs=[
            pl.BlockSpec(memory_space=pltpu.SMEM),
            pl.BlockSpec((tq, 2 * HEAD_DIM), lambda b, h, i: (b * nq + i, h)),
            pl.BlockSpec((seq, 2 * HEAD_DIM), lambda b, h, i: (b, h)),
            pl.BlockSpec((seq, HEAD_DIM), lambda b, h, i: (b, COL_FV // HEAD_DIM + h)),
        ],
        out_specs=pl.BlockSpec((tq, HEAD_DIM), lambda b, h, i: (b * nq + i, h)),
        out_shape=jax.ShapeDtypeStruct((m, MIX_W), BF16),
        scratch_shapes=[pltpu.VMEM((2, tq, tq), F32),
                        pltpu.VMEM((tq, LANES), F32), pltpu.VMEM((tq, LANES), F32),
                        pltpu.VMEM((tq, HEAD_DIM), F32),
                        pltpu.VMEM((tq, 1), F32), pltpu.VMEM((SUBLANES, LANES), F32)],
        compiler_params=_cparams("parallel", "parallel", "arbitrary"),
        name="fox_attn",
    )(cend, q_aug, k_aug, y)


def _sb_attn_kernel(q_ref, k_ref, v_ref, o_ref, zw_ref, tot_ref, after_ref, acc_ref, zb_ref,
                    kmax_ref, *, tq, tk, seq):
    @pl.when(pl.program_id(2) == 0)
    def _():
        kmax_ref[...] = jnp.broadcast_to(_max_sq_norm(k_ref, tq, seq, HEAD_DIM), kmax_ref.shape)

    qf = q_ref[...].astype(F32)
    zb_ref[...] = jnp.sqrt(jnp.sum(qf * qf, axis=-1, keepdims=True) * kmax_ref[0:1, 0:1]) * NORM_SLACK

    n_sub = tq // tk
    row_k = lax.broadcasted_iota(jnp.int32, (tk, tk), 0)
    col_k = lax.broadcasted_iota(jnp.int32, (tk, tk), 1)
    neg_suffix = jnp.where(row_k >= col_k, -1.0, 0.0).astype(BF16)
    after_ref[...] = jnp.zeros((tq, 1), F32)
    acc_ref[...] = jnp.zeros((tq, HEAD_DIM), F32)

    def valid_mask(sub):
        row = lax.broadcasted_iota(jnp.int32, (tq, tk), 0)
        col = lax.broadcasted_iota(jnp.int32, (tq, tk), 1) + sub * tk
        return col < row

    def logits(j, slot, masked):
        for sub in range(n_sub):
            k = k_ref[pl.ds(pl.multiple_of(j * tq + sub * tk, tk), tk), :]
            z = lax.dot_general(q_ref[...], k, NT_DIMS, preferred_element_type=F32)
            sp = jnp.maximum(z, 0.0) + jnp.log(1.0 + jnp.exp2(jnp.abs(z) * (-LOG2E)))
            if masked:
                sp = jnp.where(valid_mask(sub), sp, 0.0)
            within = jnp.dot(sp.astype(BF16), neg_suffix, preferred_element_type=F32)
            zw_ref[slot, sub] = z + within
            tot_ref[slot, sub] = within[:, 0:1]

    def weights_pv(j, slot, masked):
        for sub in reversed(range(n_sub)):
            v = v_ref[pl.ds(pl.multiple_of(j * tq + sub * tk, tk), tk), :]
            after = after_ref[...]
            e = zw_ref[slot, sub] + after
            if masked:
                e = jnp.where(valid_mask(sub), e, -jnp.inf)
            a = jnp.exp(e)
            acc_ref[...] += jnp.dot(a.astype(BF16), v, preferred_element_type=F32)
            after_ref[...] = after + tot_ref[slot, sub]

    def done_fn():
        return jnp.max(zb_ref[...] + after_ref[...]) < SKIP_EXP

    _checked_sweep(pl.program_id(2), logits, weights_pv, done_fn)
    o_ref[...] = acc_ref[...].astype(BF16)


def _sb_attn(y, batch, seq, tq, tk):
    m = batch * seq
    nq = seq // tq
    return pl.pallas_call(
        functools.partial(_sb_attn_kernel, tq=tq, tk=tk, seq=seq),
        grid=(batch, HEADS, nq),
        in_specs=[
            pl.BlockSpec((tq, HEAD_DIM), lambda b, h, i: (b * nq + i, COL_SQ // HEAD_DIM + h)),
            pl.BlockSpec((seq, HEAD_DIM), lambda b, h, i: (b, COL_SK // HEAD_DIM + h)),
            pl.BlockSpec((seq, HEAD_DIM), lambda b, h, i: (b, COL_SV // HEAD_DIM + h)),
        ],
        out_specs=pl.BlockSpec((tq, HEAD_DIM), lambda b, h, i: (b * nq + i, h)),
        out_shape=jax.ShapeDtypeStruct((m, MIX_W), BF16),
        scratch_shapes=[pltpu.VMEM((2, tq // tk, tq, tk), F32),
                        pltpu.VMEM((2, tq // tk, tq, 1), F32),
                        pltpu.VMEM((tq, 1), F32), pltpu.VMEM((tq, HEAD_DIM), F32),
                        pltpu.VMEM((tq, 1), F32), pltpu.VMEM((SUBLANES, LANES), F32)],
        compiler_params=_cparams("parallel", "parallel", "arbitrary"),
        name="sb_attn",
    )(y, y, y)


def _bdot(a, b):
    return jnp.dot(a.astype(BF16), b.astype(BF16), preferred_element_type=F32)


def _unit_lower_inverses(a_list, eye):
    c = eye.shape[0]
    xs = [eye - a for a in a_list]
    ps = [_bdot(a, a) for a in a_list]
    power = 2
    while 2 * power < c:
        xps = [_bdot(jnp.concatenate([x, p], axis=0), p) for x, p in zip(xs, ps)]
        xs = [x + xp[:c] for x, xp in zip(xs, xps)]
        ps = [xp[c:] for xp in xps]
        power *= 2
    return [x + _bdot(x, p) for x, p in zip(xs, ps)]


def _gdn_kernel(yq_ref, yk_ref, yv_ref, yz_ref, g_ref, cw_ref, alog_ref, bias_ref, on_ref,
                o_ref, state_ref, xs_ref, *, tg):
    @pl.when(pl.program_id(1) == 0)
    def _():
        state_ref[...] = jnp.zeros_like(state_ref)
        xs_ref[0:SUBLANES, :] = jnp.zeros((SUBLANES, 3 * MIX_W), F32)

    c = GDN_CHUNK
    xs_ref[SUBLANES:SUBLANES + tg, 0:MIX_W] = yq_ref[...].astype(F32)
    xs_ref[SUBLANES:SUBLANES + tg, MIX_W:2 * MIX_W] = yk_ref[...].astype(F32)
    xs_ref[SUBLANES:SUBLANES + tg, 2 * MIX_W:3 * MIX_W] = yv_ref[...].astype(F32)
    conv = jnp.zeros((tg, 3 * MIX_W), F32)
    for tap in range(GDN_CONV):
        off = SUBLANES - (GDN_CONV - 1) + tap
        conv = conv + cw_ref[tap:tap + 1, :] * xs_ref[off:off + tg, :]
    xs_ref[0:SUBLANES, :] = xs_ref[tg:tg + SUBLANES, :]
    qkv = _silu(conv)

    g_all = g_ref[...]
    beta_all = jax.nn.sigmoid(g_all)
    glog_all = -jnp.exp(alog_ref[...]) * _softplus(g_all + bias_ref[...])
    lane = lax.broadcasted_iota(jnp.int32, (tg, LANES), 1)
    glog_all = jnp.where((lane >= LANE_GA) & (lane < LANE_GA + HEADS), glog_all, 0.0)

    row = lax.broadcasted_iota(jnp.int32, (c, c), 0)
    col = lax.broadcasted_iota(jnp.int32, (c, c), 1)
    causal = row >= col
    strict = row > col
    eye = jnp.where(row == col, 1.0, 0.0)
    tril = jnp.where(causal, 1.0, 0.0).astype(BF16)

    items = [(n, h) for n in range(tg // c) for h in range(HEADS)]
    gc_alls, gc_rows = [], []
    for n in range(tg // c):
        g1, g2, g3 = _split3(glog_all[n * c:(n + 1) * c, :])
        gc_all = (jnp.dot(tril, g1, preferred_element_type=F32)
                  + jnp.dot(tril, g2, preferred_element_type=F32)
                  + jnp.dot(tril, g3, preferred_element_type=F32))
        gc_alls.append(gc_all)
        gc_rows.append(gc_all.T)

    pre = []
    for n, h in items:
        r0 = n * c
        cq = qkv[r0:r0 + c, h * HEAD_DIM:(h + 1) * HEAD_DIM]
        ck = qkv[r0:r0 + c, MIX_W + h * HEAD_DIM:MIX_W + (h + 1) * HEAD_DIM]
        v = qkv[r0:r0 + c, 2 * MIX_W + h * HEAD_DIM:2 * MIX_W + (h + 1) * HEAD_DIM]
        q = cq * lax.rsqrt(jnp.sum(cq * cq, axis=-1, keepdims=True) + EPS) * (HEAD_DIM ** -0.5)
        k = ck * lax.rsqrt(jnp.sum(ck * ck, axis=-1, keepdims=True) + EPS)
        beta = _lane_col(beta_all[r0:r0 + c, :], LANE_GB + h)
        gc = _lane_col(gc_alls[n], LANE_GA + h)
        gc_row = gc_rows[n][LANE_GA + h:LANE_GA + h + 1, :]
        g_last = gc[c - 1:c, :]
        decay = jnp.exp(jnp.where(causal, gc - gc_row, -jnp.inf))
        kb = k.astype(BF16)
        kk = lax.dot_general(kb, kb, NT_DIMS, preferred_element_type=F32)
        qk = lax.dot_general(q.astype(BF16), kb, NT_DIMS, preferred_element_type=F32)
        pre.append(dict(
            a_strict=jnp.where(strict, beta * kk * decay, 0.0),
            attn=jnp.where(causal, qk * decay, 0.0).astype(BF16),
            rhs=jnp.concatenate([v * beta, k * (beta * jnp.exp(gc))], axis=-1),
            q_dec=(q * jnp.exp(gc)).astype(BF16),
            k_tail=(k * jnp.exp(g_last - gc)).astype(BF16),
            s_dec=jnp.exp(g_last)))

    t_invs = _unit_lower_inverses([p["a_strict"] for p in pre], eye)
    sols = [_bdot(t, p["rhs"]) for t, p in zip(t_invs, pre)]

    for n in range(tg // c):
        r0 = n * c
        idx = [n * HEADS + h for h in range(HEADS)]
        states = [state_ref[h] for h in range(HEADS)]
        sbs = [s.astype(BF16) for s in states]
        vnbs = [(sols[i][:, :HEAD_DIM] - jnp.dot(sols[i][:, HEAD_DIM:].astype(BF16), sb,
                                                 preferred_element_type=F32)).astype(BF16)
                for i, sb in zip(idx, sbs)]
        outs = [jnp.dot(pre[i]["q_dec"], sb, preferred_element_type=F32)
                + jnp.dot(pre[i]["attn"], vnb, preferred_element_type=F32)
                for i, sb, vnb in zip(idx, sbs, vnbs)]
        for h in range(HEADS):
            i = idx[h]
            state_ref[h] = states[h] * pre[i]["s_dec"] + lax.dot_general(
                pre[i]["k_tail"], vnbs[h], TN_DIMS, preferred_element_type=F32)
            sl = slice(h * HEAD_DIM, (h + 1) * HEAD_DIM)
            z = yz_ref[r0:r0 + c, sl].astype(F32)
            o_ref[r0:r0 + c, sl] = (_rms(outs[h], on_ref[...]) * _silu(z)).astype(BF16)


def _gdn(y, g, conv_w, alog_row, bias_row, onorm, batch, seq, tg):
    m = batch * seq
    nt = seq // tg
    rows = lambda b, i: b * nt + i
    ycol = lambda off: pl.BlockSpec((tg, MIX_W), lambda b, i: (rows(b, i), off // MIX_W))
    return pl.pallas_call(
        functools.partial(_gdn_kernel, tg=tg),
        grid=(batch, nt),
        in_specs=[
            ycol(COL_GQ), ycol(COL_GK), ycol(COL_GV), ycol(COL_GZ),
            pl.BlockSpec((tg, LANES), lambda b, i: (rows(b, i), 0)),
            _const_spec((GDN_CONV, 3 * MIX_W)),
            _const_spec((1, LANES)),
            _const_spec((1, LANES)),
            _const_spec((1, HEAD_DIM)),
        ],
        out_specs=pl.BlockSpec((tg, MIX_W), lambda b, i: (rows(b, i), 0)),
        out_shape=jax.ShapeDtypeStruct((m, MIX_W), BF16),
        scratch_shapes=[pltpu.VMEM((HEADS, HEAD_DIM, HEAD_DIM), F32),
                        pltpu.VMEM((tg + SUBLANES, 3 * MIX_W), F32)],
        compiler_params=_cparams("parallel", "arbitrary"),
        name="gdn",
    )(y, y, y, y, g, conv_w, alog_row, bias_row, onorm)


def _merge_kernel(x_ref, ya_ref, yb_ref, yc_ref, g0_ref, g1_ref, g2_ref, gb_ref,
                  woa_ref, wob_ref, woc_ref, wout_ref, o_ref, *, d):
    def branch(y_ref, w_ref, gate_ref, idx):
        gate = jax.nn.sigmoid(gate_ref[...].astype(F32) + gb_ref[:, idx * d:(idx + 1) * d])
        return gate * jnp.dot(y_ref[...], w_ref[...], preferred_element_type=F32)

    mixed = (branch(ya_ref, woa_ref, g0_ref, 0) + branch(yb_ref, wob_ref, g1_ref, 1)
             + branch(yc_ref, woc_ref, g2_ref, 2))
    o_ref[...] = x_ref[...] + jnp.dot(mixed.astype(BF16), wout_ref[...], preferred_element_type=F32)


def _merge(xf, ya, yb, yc, y, gate_bias, woa, wob, woc, wout, tm):
    m, d = xf.shape
    rowblk = lambda w, cb: pl.BlockSpec((tm, w), lambda i: (i, cb))
    return pl.pallas_call(
        functools.partial(_merge_kernel, d=d),
        grid=(m // tm,),
        in_specs=[
            rowblk(d, 0), rowblk(MIX_W, 0), rowblk(MIX_W, 0), rowblk(MIX_W, 0),
            rowblk(d, COL_GATES // d), rowblk(d, COL_GATES // d + 1), rowblk(d, COL_GATES // d + 2),
            _const_spec((1, 3 * d)),
            _const_spec((MIX_W, d)), _const_spec((MIX_W, d)), _const_spec((MIX_W, d)),
            _const_spec((d, d)),
        ],
        out_specs=rowblk(d, 0),
        out_shape=jax.ShapeDtypeStruct((m, d), F32),
        compiler_params=_cparams("parallel"),
        name="merge_out",
    )(xf, ya, yb, yc, y, y, y, gate_bias, woa, wob, woc, wout)


def _mem_kv_kernel(m_ref, g_ref, wkv_ref, kn_ref, k_ref, v_ref):
    mn = _rms(m_ref[...], g_ref[...]).astype(BF16)
    kv = jnp.dot(mn, wkv_ref[...], preferred_element_type=F32)
    for h in range(HEADS):
        sl = slice(h * HEAD_DIM, (h + 1) * HEAD_DIM)
        k_ref[:, sl] = _rms(kv[:, sl], kn_ref[...]).astype(BF16)
    v_ref[...] = kv[:, MIX_W:].astype(BF16)


def _mem_kv(memf, gain, wkv, knorm, batch, n_mem):
    d = memf.shape[1]
    return pl.pallas_call(
        _mem_kv_kernel,
        grid=(batch,),
        in_specs=[
            pl.BlockSpec((n_mem, d), lambda b: (b, 0)),
            _const_spec((1, d)),
            _const_spec((d, 2 * MIX_W)),
            _const_spec((1, HEAD_DIM)),
        ],
        out_specs=[pl.BlockSpec((n_mem, MIX_W), lambda b: (b, 0))] * 2,
        out_shape=[jax.ShapeDtypeStruct((batch * n_mem, MIX_W), BF16)] * 2,
        compiler_params=_cparams("parallel"),
        name="mem_kv",
    )(memf, gain, wkv, knorm)


def _mem_attn_kernel(x_ref, g_ref, wq_ref, qn_ref, k_ref, v_ref, wo_ref, o_ref):
    x = x_ref[...]
    hq = _rms(x, g_ref[...]).astype(BF16)
    q_all = jnp.dot(hq, wq_ref[...], preferred_element_type=F32)
    outs = []
    for h in range(HEADS):
        sl = slice(h * HEAD_DIM, (h + 1) * HEAD_DIM)
        q = (_rms(q_all[:, sl], qn_ref[...]) * (HEAD_DIM ** -0.5)).astype(BF16)
        s = lax.dot_general(q, k_ref[:, sl], NT_DIMS, preferred_element_type=F32)
        p = jnp.exp(s - jnp.max(s, axis=-1, keepdims=True))
        p = p / jnp.sum(p, axis=-1, keepdims=True)
        outs.append(jnp.dot(p.astype(BF16), v_ref[:, sl], preferred_element_type=F32).astype(BF16))
    o = jnp.concatenate(outs, axis=-1)
    o_ref[...] = x + jnp.dot(o, wo_ref[...], preferred_element_type=F32)


def _mem_attn(xf, gain, wq, qnorm, kmem, vmem, wo, batch, seq, n_mem, tm):
    m, d = xf.shape
    nt = seq // tm
    return pl.pallas_call(
        _mem_attn_kernel,
        grid=(batch, nt),
        in_specs=[
            pl.BlockSpec((tm, d), lambda b, i: (b * nt + i, 0)),
            _const_spec((1, d)),
            _const_spec((d, MIX_W)),
            _const_spec((1, HEAD_DIM)),
            pl.BlockSpec((n_mem, MIX_W), lambda b, i: (b, 0)),
            pl.BlockSpec((n_mem, MIX_W), lambda b, i: (b, 0)),
            _const_spec((MIX_W, d)),
        ],
        out_specs=pl.BlockSpec((tm, d), lambda b, i: (b * nt + i, 0)),
        out_shape=jax.ShapeDtypeStruct((m, d), F32),
        compiler_params=_cparams("parallel", "parallel"),
        name="mem_attn",
    )(xf, gain, wq, qnorm, kmem, vmem, wo)


def _ffn_kernel(x_ref, g_ref, wup_ref, cw_ref, cb_ref, wdn_ref, o_ref, carry_ref, xs_ref,
                *, tm, d_ff, n_chunk):
    @pl.when(pl.program_id(1) == 0)
    def _():
        carry_ref[...] = jnp.zeros_like(carry_ref)

    x = x_ref[...]
    h = _rms(x, g_ref[...]).astype(BF16)

    def conv_half(c0):
        u = jnp.dot(h, wup_ref[:, c0:c0 + n_chunk], preferred_element_type=F32)
        xs_ref[0:SUBLANES, :] = carry_ref[:, c0:c0 + n_chunk]
        xs_ref[SUBLANES:SUBLANES + tm, :] = u
        carry_ref[:, c0:c0 + n_chunk] = xs_ref[tm:tm + SUBLANES, :]
        out = jnp.broadcast_to(cb_ref[:, c0:c0 + n_chunk], (tm, n_chunk))
        for tap in range(FFN_CONV):
            off = SUBLANES - (FFN_CONV - 1) + tap
            out = out + cw_ref[tap:tap + 1, c0:c0 + n_chunk] * xs_ref[off:off + tm, :]
        return out

    acc = x
    for c0 in range(0, d_ff, n_chunk):
        a = conv_half(c0)
        b = conv_half(d_ff + c0)
        act = (_silu(a) * b).astype(BF16)
        acc = acc + jnp.dot(act, wdn_ref[c0:c0 + n_chunk, :], preferred_element_type=F32)
    o_ref[...] = acc


def _ffn(xf, gain, wup, conv_w, conv_b, wdn, batch, seq, tm, n_chunk):
    m, d = xf.shape
    d_ff = wdn.shape[0]
    nt = seq // tm
    return pl.pallas_call(
        functools.partial(_ffn_kernel, tm=tm, d_ff=d_ff, n_chunk=n_chunk),
        grid=(batch, nt),
        in_specs=[
            pl.BlockSpec((tm, d), lambda b, i: (b * nt + i, 0)),
            _const_spec((1, d)),
            _const_spec((d, 2 * d_ff)),
            _const_spec((FFN_CONV, 2 * d_ff)),
            _const_spec((1, 2 * d_ff)),
            _const_spec((d_ff, d)),
        ],
        out_specs=pl.BlockSpec((tm, d), lambda b, i: (b * nt + i, 0)),
        out_shape=jax.ShapeDtypeStruct((m, d), F32),
        scratch_shapes=[pltpu.VMEM((SUBLANES, 2 * d_ff), F32),
                        pltpu.VMEM((tm + SUBLANES, n_chunk), F32)],
        compiler_params=_cparams("parallel", "arbitrary"),
        name="conv_ffn",
    )(xf, gain, wup, conv_w, conv_b, wdn)


def _lane_row(values, lane0):
    return jnp.zeros((1, LANES), F32).at[0, lane0:lane0 + values.shape[0]].set(values.astype(F32))


def _pack_w_in(w):
    splits = (512, 512, 512, 4, 512, 512, 512, 4, 4, 512, 512, 512, 512)
    parts, off = [], 0
    for s in splits:
        parts.append(w[:, off:off + s])
        off += s
    gates = w[:, off:]
    fq, fk, fv, ff, gq, gk, gv, gb, ga, gz, sq, sk, sv = parts
    sq = sq * (HEAD_DIM ** -0.5)
    big =jnp.concatenate([fq, fk, fv, gq, gk, gv, gz, sq, sk, sv, gates], axis=1).astype(BF16)
    small = jnp.concatenate(
        [ff, gb, ga, jnp.zeros((w.shape[0], LANES - 3 * HEADS), w.dtype)], axis=1).astype(BF16)
    return big, small


def _tile(n, pref):
    return pref if n % pref == 0 else n


def kernel(x, mem, norm_mix, w_in, fox_fbias, fox_qnorm, fox_knorm, gdn_conv, gdn_a_log, gdn_dt_bias, gdn_onorm, gate_bias, w_oa, w_ob, w_oc, w_out, norm_xq, norm_mem, w_mq, w_mkv, mq_norm, mk_norm, w_mo, norm_ffn, w_up, ffn_conv, ffn_conv_b, w_down):
    batch, seq, d = x.shape
    n_mem = mem.shape[1]
    depth = w_in.shape[0]
    d_ff = w_down.shape[1]
    m = batch * seq
    assert d == 1024 and w_in.shape[2] == N_BIG + 3 * HEADS and seq % GDN_CHUNK == 0

    tm = _tile(seq, 512)
    tq = _tile(seq, 512)
    tk_sb = _tile(tq, 256)
    t_prep = _tile(seq, 256)
    tg = _tile(seq, 128)
    n_chunk = d_ff // 2

    xf = x.reshape(m, d)
    memf = mem.reshape(batch * n_mem, d)
    row = lambda v: v.reshape(1, -1).astype(F32)
    for l in range(depth):
        w_big, w_small = _pack_w_in(w_in[l])
        y, g = _inproj(xf, row(norm_mix[l]), w_big, w_small, tm)

        q_aug, k_aug, c_ends = _fox_prep(y, g, _lane_row(fox_fbias[l], LANE_FF), row(fox_qnorm[l]),
                                         row(fox_knorm[l]), batch, seq, t_prep)
        step = tq // t_prep
        cend = c_ends[:, 0, LANE_FF:LANE_FF + HEADS].reshape(batch, seq // t_prep, HEADS)
        cend = cend[:, step - 1::step, :].transpose(0, 2, 1).reshape(batch * HEADS, seq // tq)
        ya = _fox_attn(cend, q_aug, k_aug, y, batch, seq, tq)
        yb = _gdn(y, g, gdn_conv[l].astype(F32), _lane_row(gdn_a_log[l], LANE_GA),
                  _lane_row(gdn_dt_bias[l], LANE_GA), row(gdn_onorm[l]), batch, seq, tg)
        yc = _sb_attn(y, batch, seq, tq, tk_sb)

        xf = _merge(xf, ya, yb, yc, y, row(gate_bias[l]), w_oa[l].astype(BF16),
                    w_ob[l].astype(BF16), w_oc[l].astype(BF16), w_out[l].astype(BF16), tm)

        kmem, vmem = _mem_kv(memf, row(norm_mem[l]), w_mkv[l].astype(BF16), row(mk_norm[l]),
                             batch, n_mem)
        xf = _mem_attn(xf, row(norm_xq[l]), w_mq[l].astype(BF16), row(mq_norm[l]), kmem, vmem,
                       w_mo[l].astype(BF16), batch, seq, n_mem, tm)

        xf = _ffn(xf, row(norm_ffn[l]), w_up[l].astype(BF16), ffn_conv[l].astype(F32),
                  row(ffn_conv_b[l]), w_down[l].astype(BF16), batch, seq, tm, n_chunk)
    return xf.reshape(batch, seq, d)
```

```python
import functools

import jax
import jax.numpy as jnp
from jax import lax
from jax.experimental import pallas as pl
from jax.experimental.pallas import tpu as pltpu

F32 = jnp.float32
BF16 = jnp.bfloat16
EPS = 1e-6
LOG2E = 1.4426950408889634
SKIP_EXP = -110.0
SKIP_EXP2 = -155.0
NORM_SLACK = 1.01

HEADS = 4
HEAD_DIM = 128
MIX_W = HEADS * HEAD_DIM
GDN_CHUNK = 64
GDN_CONV = 4
FFN_CONV = 3
LANES = 128
SUBLANES = 8
VMEM_LIMIT = 56 * 1024 * 1024

COL_FQ, COL_FK, COL_FV = 0, 512, 1024
COL_GQ, COL_GK, COL_GV, COL_GZ = 1536, 2048, 2560, 3072
COL_SQ, COL_SK, COL_SV = 3584, 4096, 4608
COL_GATES = 5120
N_BIG = 8192
LANE_FF, LANE_GB, LANE_GA = 0, 4, 8

NT_DIMS = (((1,), (1,)), ((), ()))
TN_DIMS = (((0,), (0,)), ((), ()))


def _cparams(*sem):
    return pltpu.CompilerParams(dimension_semantics=sem, vmem_limit_bytes=VMEM_LIMIT)


def _const_spec(shape):
    nd = len(shape)
    return pl.BlockSpec(shape, lambda *_: (0,) * nd, pipeline_mode=pl.Buffered(1))


def _rms(xf, gain):
    return xf * lax.rsqrt(jnp.mean(xf * xf, axis=-1, keepdims=True) + EPS) * gain


def _softplus(x):
    return jnp.maximum(x, 0.0) + jnp.log1p(jnp.exp(-jnp.abs(x)))


def _silu(x):
    h = 0.5 * x
    return h + h * jnp.tanh(h)


def _split3(x):
    a = x.astype(BF16)
    r = x - a.astype(F32)
    b = r.astype(BF16)
    c = (r - b.astype(F32)).astype(BF16)
    return a, b, c


def _lane_col(x, lane):
    idx = lax.broadcasted_iota(jnp.int32, x.shape, 1)
    return jnp.sum(jnp.where(idx == lane, x, 0.0), axis=-1, keepdims=True)


def _inproj_kernel(x_ref, g_ref, w_ref, ws_ref, y_ref, gs_ref, *, n_chunk):
    h = _rms(x_ref[...], g_ref[...]).astype(BF16)
    for c in range(0, N_BIG, n_chunk):
        y_ref[:, c:c + n_chunk] = jnp.dot(
            h, w_ref[:, c:c + n_chunk], preferred_element_type=F32).astype(BF16)
    gs_ref[...] = jnp.dot(h, ws_ref[...], preferred_element_type=F32)


def _inproj(xf, gain, w_big, w_small, tm):
    m, d = xf.shape
    return pl.pallas_call(
        functools.partial(_inproj_kernel, n_chunk=512),
        grid=(m // tm,),
        in_specs=[
            pl.BlockSpec((tm, d), lambda i: (i, 0)),
            _const_spec((1, d)),
            _const_spec((d, N_BIG)),
            _const_spec((d, LANES)),
        ],
        out_specs=[
            pl.BlockSpec((tm, N_BIG), lambda i: (i, 0)),
            pl.BlockSpec((tm, LANES), lambda i: (i, 0)),
        ],
        out_shape=[jax.ShapeDtypeStruct((m, N_BIG), BF16),
                   jax.ShapeDtypeStruct((m, LANES), F32)],
        compiler_params=_cparams("parallel"),
        name="inproj",
    )(xf, gain, w_big, w_small)


def _fox_prep_kernel(yq_ref, yk_ref, g_ref, fb_ref, qn_ref, kn_ref, qa_ref, ka_ref, ce_ref,
                     carry_ref, *, t):
    @pl.when(pl.program_id(1) == 0)
    def _():
        carry_ref[...] = jnp.zeros_like(carry_ref)

    lane = lax.broadcasted_iota(jnp.int32, (t, LANES), 1)
    x = g_ref[...] + fb_ref[...]
    logf = jnp.minimum(x, 0.0) - jnp.log1p(jnp.exp(-jnp.abs(x)))
    logf = jnp.where(lane < LANE_FF + HEADS, logf, 0.0)
    row = lax.broadcasted_iota(jnp.int32, (t, t), 0)
    col = lax.broadcasted_iota(jnp.int32, (t, t), 1)
    tri = jnp.where(row >= col, 1.0, 0.0).astype(BF16)
    l1, l2, l3 = _split3(logf)
    c_all = (jnp.dot(tri, l1, preferred_element_type=F32)
             + jnp.dot(tri, l2, preferred_element_type=F32)
             + jnp.dot(tri, l3, preferred_element_type=F32)) + carry_ref[0:1, :]
    carry_ref[0:1, :] = c_all[t - 1:t, :]
    ce_ref[0] = jnp.broadcast_to(c_all[t - 1:t, :] * LOG2E, (SUBLANES, LANES))

    scale = HEAD_DIM ** -0.5 * LOG2E
    for h in range(HEADS):
        sl = slice(h * HEAD_DIM, (h + 1) * HEAD_DIM)
        qn = _rms(yq_ref[:, sl].astype(F32), qn_ref[...]) * scale
        kn = _rms(yk_ref[:, sl].astype(F32), kn_ref[...])
        cb = jnp.broadcast_to(_lane_col(c_all, LANE_FF + h) * LOG2E, (t, LANES))
        c1, c2, c3 = (p.astype(F32) for p in _split3(cb))
        aux_q = jnp.where(lane == 0, c1, jnp.where(lane == 1, c2, jnp.where(
            lane == 2, c3, jnp.where(lane < 6, 1.0, 0.0))))
        aux_k = jnp.where(lane < 3, 1.0, jnp.where(lane == 3, -c1, jnp.where(
            lane == 4, -c2, jnp.where(lane == 5, -c3, 0.0))))
        base = 2 * h * HEAD_DIM
        qa_ref[:, base:base + HEAD_DIM] = qn.astype(BF16)
        qa_ref[:, base + HEAD_DIM:base + 2 * HEAD_DIM] = aux_q.astype(BF16)
        ka_ref[:, base:base + HEAD_DIM] = kn.astype(BF16)
        ka_ref[:, base + HEAD_DIM:base + 2 * HEAD_DIM] = aux_k.astype(BF16)


def _fox_prep(y, g, fb_row, qn, kn, batch, seq, t):
    m = batch * seq
    nt = seq // t
    rows = lambda b, i: b * nt + i
    return pl.pallas_call(
        functools.partial(_fox_prep_kernel, t=t),
        grid=(batch, nt),
        in_specs=[
            pl.BlockSpec((t, MIX_W), lambda b, i: (rows(b, i), COL_FQ // MIX_W)),
            pl.BlockSpec((t, MIX_W), lambda b, i: (rows(b, i), COL_FK // MIX_W)),
            pl.BlockSpec((t, LANES), lambda b, i: (rows(b, i), 0)),
            _const_spec((1, LANES)),
            _const_spec((1, HEAD_DIM)),
            _const_spec((1, HEAD_DIM)),
        ],
        out_specs=[
            pl.BlockSpec((t, 2 * MIX_W), lambda b, i: (rows(b, i), 0)),
            pl.BlockSpec((t, 2 * MIX_W), lambda b, i: (rows(b, i), 0)),
            pl.BlockSpec((1, SUBLANES, LANES), lambda b, i: (rows(b, i), 0, 0)),
        ],
        out_shape=[jax.ShapeDtypeStruct((m, 2 * MIX_W), BF16)] * 2
        + [jax.ShapeDtypeStruct((batch * nt, SUBLANES, LANES), F32)],
        scratch_shapes=[pltpu.VMEM((SUBLANES, LANES), F32)],
        compiler_params=_cparams("parallel", "arbitrary"),
        name="fox_prep",
    )(y, y, g, fb_row, qn, kn)


def _bounded_sweep(i, stage_a, stage_b, first_tile_fn):
    stage_a(i, 0, True)

    @pl.when(i == 0)
    def _():
        stage_b(i, 0, True)

    @pl.when(i > 0)
    def _():
        stage_a(i - 1, 1, False)
        stage_b(i, 0, True)
        lo = first_tile_fn()
        n = i - lo
        n_pairs = (n - 1) // 2

        def body(p, carry):
            t = i - 1 - 2 * p
            stage_a(t - 1, 0, False)
            stage_b(t, 1, False)
            stage_a(t - 2, 1, False)
            stage_b(t - 1, 0, False)
            return carry

        lax.fori_loop(0, n_pairs, body, 0)
        t = i - 1 - 2 * n_pairs

        @pl.when(jnp.logical_and(n >= 1, t == lo))
        def _():
            stage_b(t, 1, False)

        @pl.when(jnp.logical_and(n >= 1, t == lo + 1))
        def _():
            stage_a(lo, 0, False)
            stage_b(t, 1, False)
            stage_b(lo, 0, False)


def _checked_sweep(i, stage_a, stage_b, last_fn):
    stage_a(i, 0, True)

    @pl.when(i == 0)
    def _():
        stage_b(i, 0, True)

    @pl.when(i > 0)
    def _():
        last_0 = last_fn(0)

        @pl.when(last_0)
        def _():
            stage_b(i, 0, True)

        @pl.when(jnp.logical_not(last_0))
        def _():
            stage_a(i - 1, 1, False)
            stage_b(i, 0, True)

            def body(carry):
                t = carry[0]
                last_1 = jnp.logical_or(t == 0, last_fn(1))

                @pl.when(last_1)
                def _():
                    stage_b(t, 1, False)

                @pl.when(jnp.logical_not(last_1))
                def _():
                    stage_a(t - 1, 0, False)
                    stage_b(t, 1, False)

                last_2 = jnp.logical_or(t == 1, last_fn(0))
                more = jnp.logical_not(last_1)

                @pl.when(jnp.logical_and(more, last_2))
                def _():
                    stage_b(t - 1, 0, False)

                @pl.when(jnp.logical_and(more, jnp.logical_not(last_2)))
                def _():
                    stage_a(t - 2, 1, False)
                    stage_b(t - 1, 0, False)

                return t - 2, jnp.logical_or(last_1, last_2)

            lax.while_loop(lambda carry: jnp.logical_not(carry[1]), body, (i - 1, last_0))


def _max_sq_norm(ref, tq, seq, width):
    def chunk(c, best):
        x = ref[pl.ds(pl.multiple_of(c * tq, tq), tq), 0:width].astype(F32)
        return jnp.maximum(best, jnp.sum(x * x, axis=-1, keepdims=True))

    best = lax.fori_loop(0, seq // tq, chunk, jnp.zeros((tq, 1), F32))
    return jnp.max(best, axis=0, keepdims=True)


def _fox_attn_kernel(cend_ref, q_ref, k_ref, v_ref, o_ref, s_ref, m_ref, l_ref, acc_ref, zb_ref,
                     kmax_ref, *, tq, seq):
    @pl.when(pl.program_id(2) == 0)
    def _():
        kmax_ref[...] = jnp.broadcast_to(_max_sq_norm(k_ref, tq, seq, HEAD_DIM), kmax_ref.shape)

    m_ref[...] = jnp.full((tq, LANES), -jnp.inf, F32)
    l_ref[...] = jnp.zeros((tq, LANES), F32)
    acc_ref[...] = jnp.zeros((tq, HEAD_DIM), F32)

    qf = q_ref[:, 0:HEAD_DIM].astype(F32)
    aux = q_ref[:, HEAD_DIM:2 * HEAD_DIM].astype(F32)
    lane = lax.broadcasted_iota(jnp.int32, (tq, LANES), 1)
    c_q = jnp.sum(jnp.where(lane < 3, aux, 0.0), axis=-1, keepdims=True)
    q_sq = jnp.sum(qf * qf, axis=-1, keepdims=True)
    zb_ref[...] = jnp.sqrt(q_sq * kmax_ref[0:1, 0:1]) * NORM_SLACK + c_q

    def scores(j, slot, masked):
        del masked
        k = k_ref[pl.ds(pl.multiple_of(j * tq, tq), tq), :]
        s_ref[slot] = lax.dot_general(q_ref[...], k, NT_DIMS, preferred_element_type=F32)

    def softmax_pv(j, slot, masked):
        v = v_ref[pl.ds(pl.multiple_of(j * tq, tq), tq), :]
        s = s_ref[slot]
        if masked:
            row = lax.broadcasted_iota(jnp.int32, (tq, tq), 0)
            col = lax.broadcasted_iota(jnp.int32, (tq, tq), 1)
            s = jnp.where(col <= row, s, -jnp.inf)
        m_prev = m_ref[...]
        m_new = jnp.maximum(m_prev, jnp.max(s, axis=-1, keepdims=True))
        p = jnp.exp2(s - jnp.concatenate([m_new] * (tq // LANES), axis=1))
        alpha = jnp.exp2(m_prev - m_new)
        l_ref[...] = alpha * l_ref[...] + jnp.sum(p, axis=-1, keepdims=True)
        acc_ref[...] = alpha * acc_ref[...] + jnp.dot(p.astype(BF16), v, preferred_element_type=F32)
        m_ref[...] = m_new

    def first_tile_fn():
        slack = jnp.max(zb_ref[...] - m_ref[...])
        row = pl.program_id(0) * HEADS + pl.program_id(1)

        def count(j, lo):
            return lo + jnp.where(slack - cend_ref[row, j] < SKIP_EXP2, 1, 0)

        return lax.fori_loop(0, pl.program_id(2), count, 0)

    _bounded_sweep(pl.program_id(2), scores, softmax_pv, first_tile_fn)
    o_ref[...] = (acc_ref[...] / l_ref[...]).astype(BF16)


def _fox_attn(cend, q_aug, k_aug, y, batch, seq, tq):
    m = batch * seq
    nq = seq // tq
    return pl.pallas_call(
        functools.partial(_fox_attn_kernel, tq=tq, seq=seq),
        grid=(batch, HEADS, nq),
        in_specs=[
            pl.BlockSpec(memory_space=pltpu.SMEM),
            pl.BlockSpec((tq, 2 * HEAD_DIM), lambda b, h, i: (b * nq + i, h)),
            pl.BlockSpec((seq, 2 * HEAD_DIM), lambda b, h, i: (b, h)),
            pl.BlockSpec((seq, HEAD_DIM), lambda b, h, i: (b, COL_FV // HEAD_DIM + h)),
        ],
        out_specs=pl.BlockSpec((tq, HEAD_DIM), lambda b, h, i: (b * nq + i, h)),
        out_shape=jax.ShapeDtypeStruct((m, MIX_W), BF16),
        scratch_shapes=[pltpu.VMEM((2, tq, tq), F32),
                        pltpu.VMEM((tq, LANES), F32), pltpu.VMEM((tq, LANES), F32),
                        pltpu.VMEM((tq, HEAD_DIM), F32),
                        pltpu.VMEM((tq, 1), F32), pltpu.VMEM((SUBLANES, LANES), F32)],
        compiler_params=_cparams("parallel", "parallel", "arbitrary"),
        name="fox_attn",
    )(cend, q_aug, k_aug, y)


def _sb_attn_kernel(q_ref, k_ref, v_ref, o_ref, zw_ref, tot_ref, after_ref, acc_ref, zb_ref,
                    kmax_ref, *, tq, tk, seq):
    @pl.when(pl.program_id(2) == 0)
    def _():
        kmax_ref[...] = jnp.broadcast_to(_max_sq_norm(k_ref, tq, seq, HEAD_DIM), kmax_ref.shape)

    qf = q_ref[...].astype(F32)
    zb_ref[...] = jnp.sqrt(jnp.sum(qf * qf, axis=-1, keepdims=True) * kmax_ref[0:1, 0:1]) * NORM_SLACK

    n_sub = tq // tk
    row_k = lax.broadcasted_iota(jnp.int32, (tk, tk), 0)
    col_k = lax.broadcasted_iota(jnp.int32, (tk, tk), 1)
    neg_suffix = jnp.where(row_k >= col_k, -1.0, 0.0).astype(BF16)
    after_ref[...] = jnp.zeros((tq, 1), F32)
    acc_ref[...] = jnp.zeros((tq, HEAD_DIM), F32)

    def valid_mask(sub):
        row = lax.broadcasted_iota(jnp.int32, (tq, tk), 0)
        col = lax.broadcasted_iota(jnp.int32, (tq, tk), 1) + sub * tk
        return col < row

    def logits(j, slot, masked):
        for sub in range(n_sub):
            k = k_ref[pl.ds(pl.multiple_of(j * tq + sub * tk, tk), tk), :]
            z = lax.dot_general(q_ref[...], k, NT_DIMS, preferred_element_type=F32)
            sp = jnp.maximum(z, 0.0) + jnp.log(1.0 + jnp.exp2(jnp.abs(z) * (-LOG2E)))
            if masked:
                sp = jnp.where(valid_mask(sub), sp, 0.0)
            within = jnp.dot(sp.astype(BF16), neg_suffix, preferred_element_type=F32)
            zw_ref[slot, sub] = z + within
            tot_ref[slot, sub] = within[:, 0:1]

    def weights_pv(j, slot, masked):
        for sub in reversed(range(n_sub)):
            v = v_ref[pl.ds(pl.multiple_of(j * tq + sub * tk, tk), tk), :]
            after = after_ref[...]
            e = zw_ref[slot, sub] + after
            if masked:
                e = jnp.where(valid_mask(sub), e, -jnp.inf)
            a = jnp.exp(e)
            acc_ref[...] += jnp.dot(a.astype(BF16), v, preferred_element_type=F32)
            after_ref[...] = after + tot_ref[slot, sub]

    def last_fn(slot):
        after = after_ref[...]
        for sub in reversed(range(n_sub)):
            after = after + tot_ref[slot, sub]
        return jnp.max(zb_ref[...] + after) < SKIP_EXP

    _checked_sweep(pl.program_id(2), logits, weights_pv, last_fn)
    o_ref[...] = acc_ref[...].astype(BF16)


def _sb_attn(y, batch, seq, tq, tk):
    m = batch * seq
    nq = seq // tq
    return pl.pallas_call(
        functools.partial(_sb_attn_kernel, tq=tq, tk=tk, seq=seq),
        grid=(batch, HEADS, nq),
        in_specs=[
            pl.BlockSpec((tq, HEAD_DIM), lambda b, h, i: (b * nq + i, COL_SQ // HEAD_DIM + h)),
            pl.BlockSpec((seq, HEAD_DIM), lambda b, h, i: (b, COL_SK // HEAD_DIM + h)),
            pl.BlockSpec((seq, HEAD_DIM), lambda b, h, i: (b, COL_SV // HEAD_DIM + h)),
        ],
        out_specs=pl.BlockSpec((tq, HEAD_DIM), lambda b, h, i: (b * nq + i, h)),
        out_shape=jax.ShapeDtypeStruct((m, MIX_W), BF16),
        scratch_shapes=[pltpu.VMEM((2, tq // tk, tq, tk), F32),
                        pltpu.VMEM((2, tq // tk, tq, 1), F32),
                        pltpu.VMEM((tq, 1), F32), pltpu.VMEM((tq, HEAD_DIM), F32),
                        pltpu.VMEM((tq, 1), F32), pltpu.VMEM((SUBLANES, LANES), F32)],
        compiler_params=_cparams("parallel", "parallel", "arbitrary"),
        name="sb_attn",
    )(y, y, y)


def _bdot(a, b):
    return jnp.dot(a.astype(BF16), b.astype(BF16), preferred_element_type=F32)


def _unit_lower_inverses(a_list, eye):
    c = eye.shape[0]
    xs = [eye - a for a in a_list]
    ps = [_bdot(a, a) for a in a_list]
    power = 2
    while 2 * power < c:
        xps = [_bdot(jnp.concatenate([x, p], axis=0), p) for x, p in zip(xs, ps)]
        xs = [x + xp[:c] for x, xp in zip(xs, xps)]
        ps = [xp[c:] for xp in xps]
        power *= 2
    return [x + _bdot(x, p) for x, p in zip(xs, ps)]


def _gdn_kernel(yq_ref, yk_ref, yv_ref, yz_ref, g_ref, cw_ref, alog_ref, bias_ref, on_ref,
                o_ref, state_ref, xs_ref, *, tg):
    @pl.when(pl.program_id(1) == 0)
    def _():
        state_ref[...] = jnp.zeros_like(state_ref)
        xs_ref[0:SUBLANES, :] = jnp.zeros((SUBLANES, 3 * MIX_W), F32)

    c = GDN_CHUNK
    xs_ref[SUBLANES:SUBLANES + tg, 0:MIX_W] = yq_ref[...].astype(F32)
    xs_ref[SUBLANES:SUBLANES + tg, MIX_W:2 * MIX_W] = yk_ref[...].astype(F32)
    xs_ref[SUBLANES:SUBLANES + tg, 2 * MIX_W:3 * MIX_W] = yv_ref[...].astype(F32)
    conv = jnp.zeros((tg, 3 * MIX_W), F32)
    for tap in range(GDN_CONV):
        off = SUBLANES - (GDN_CONV - 1) + tap
        conv = conv + cw_ref[tap:tap + 1, :] * xs_ref[off:off + tg, :]
    xs_ref[0:SUBLANES, :] = xs_ref[tg:tg + SUBLANES, :]
    qkv = _silu(conv)

    g_all = g_ref[...]
    beta_all = jax.nn.sigmoid(g_all)
    glog_all = -jnp.exp(alog_ref[...]) * _softplus(g_all + bias_ref[...])
    lane = lax.broadcasted_iota(jnp.int32, (tg, LANES), 1)
    glog_all = jnp.where((lane >= LANE_GA) & (lane < LANE_GA + HEADS), glog_all, 0.0)

    row = lax.broadcasted_iota(jnp.int32, (c, c), 0)
    col = lax.broadcasted_iota(jnp.int32, (c, c), 1)
    causal = row >= col
    strict = row > col
    eye = jnp.where(row == col, 1.0, 0.0)
    tril = jnp.where(causal, 1.0, 0.0).astype(BF16)

    items = [(n, h) for n in range(tg // c) for h in range(HEADS)]
    gc_alls, gc_rows = [], []
    for n in range(tg // c):
        g1, g2, g3 = _split3(glog_all[n * c:(n + 1) * c, :])
        gc_all = (jnp.dot(tril, g1, preferred_element_type=F32)
                  + jnp.dot(tril, g2, preferred_element_type=F32)
                  + jnp.dot(tril, g3, preferred_element_type=F32))
        gc_alls.append(gc_all)
        gc_rows.append(gc_all.T)

    pre = []
    for n, h in items:
        r0 = n * c
        cq = qkv[r0:r0 + c, h * HEAD_DIM:(h + 1) * HEAD_DIM]
        ck = qkv[r0:r0 + c, MIX_W + h * HEAD_DIM:MIX_W + (h + 1) * HEAD_DIM]
        v = qkv[r0:r0 + c, 2 * MIX_W + h * HEAD_DIM:2 * MIX_W + (h + 1) * HEAD_DIM]
        q = cq * lax.rsqrt(jnp.sum(cq * cq, axis=-1, keepdims=True) + EPS) * (HEAD_DIM ** -0.5)
        k = ck * lax.rsqrt(jnp.sum(ck * ck, axis=-1, keepdims=True) + EPS)
        beta = _lane_col(beta_all[r0:r0 + c, :], LANE_GB + h)
        gc = _lane_col(gc_alls[n], LANE_GA + h)
        gc_row = gc_rows[n][LANE_GA + h:LANE_GA + h + 1, :]
        g_last = gc[c - 1:c, :]
        decay = jnp.exp(jnp.where(causal, gc - gc_row, -jnp.inf))
        kb = k.astype(BF16)
        kk = lax.dot_general(kb, kb, NT_DIMS, preferred_element_type=F32)
        qk = lax.dot_general(q.astype(BF16), kb, NT_DIMS, preferred_element_type=F32)
        pre.append(dict(
            a_strict=jnp.where(strict, beta * kk * decay, 0.0),
            attn=jnp.where(causal, qk * decay, 0.0).astype(BF16),
            rhs=jnp.concatenate([v * beta, k * (beta * jnp.exp(gc))], axis=-1),
            q_dec=(q * jnp.exp(gc)).astype(BF16),
            k_tail=(k * jnp.exp(g_last - gc)).astype(BF16),
            s_dec=jnp.exp(g_last)))

    t_invs = _unit_lower_inverses([p["a_strict"] for p in pre], eye)
    sols = [_bdot(t, p["rhs"]) for t, p in zip(t_invs, pre)]

    for n in range(tg // c):
        r0 = n * c
        idx = [n * HEADS + h for h in range(HEADS)]
        states = [state_ref[h] for h in range(HEADS)]
        sbs = [s.astype(BF16) for s in states]
        vnbs = [(sols[i][:, :HEAD_DIM] - jnp.dot(sols[i][:, HEAD_DIM:].astype(BF16), sb,
                                                 preferred_element_type=F32)).astype(BF16)
                for i, sb in zip(idx, sbs)]
        outs = [jnp.dot(pre[i]["q_dec"], sb, preferred_element_type=F32)
                + jnp.dot(pre[i]["attn"], vnb, preferred_element_type=F32)
                for i, sb, vnb in zip(idx, sbs, vnbs)]
        for h in range(HEADS):
            i = idx[h]
            state_ref[h] = states[h] * pre[i]["s_dec"] + lax.dot_general(
                pre[i]["k_tail"], vnbs[h], TN_DIMS, preferred_element_type=F32)
            sl = slice(h * HEAD_DIM, (h + 1) * HEAD_DIM)
            z = yz_ref[r0:r0 + c, sl].astype(F32)
            o_ref[r0:r0 + c, sl] = (_rms(outs[h], on_ref[...]) * _silu(z)).astype(BF16)


def _gdn(y, g, conv_w, alog_row, bias_row, onorm, batch, seq, tg):
    m = batch * seq
    nt = seq // tg
    rows = lambda b, i: b * nt + i
    ycol = lambda off: pl.BlockSpec((tg, MIX_W), lambda b, i: (rows(b, i), off // MIX_W))
    return pl.pallas_call(
        functools.partial(_gdn_kernel, tg=tg),
        grid=(batch, nt),
        in_specs=[
            ycol(COL_GQ), ycol(COL_GK), ycol(COL_GV), ycol(COL_GZ),
            pl.BlockSpec((tg, LANES), lambda b, i: (rows(b, i), 0)),
            _const_spec((GDN_CONV, 3 * MIX_W)),
            _const_spec((1, LANES)),
            _const_spec((1, LANES)),
            _const_spec((1, HEAD_DIM)),
        ],
        out_specs=pl.BlockSpec((tg, MIX_W), lambda b, i: (rows(b, i), 0)),
        out_shape=jax.ShapeDtypeStruct((m, MIX_W), BF16),
        scratch_shapes=[pltpu.VMEM((HEADS, HEAD_DIM, HEAD_DIM), F32),
                        pltpu.VMEM((tg + SUBLANES, 3 * MIX_W), F32)],
        compiler_params=_cparams("parallel", "arbitrary"),
        name="gdn",
    )(y, y, y, y, g, conv_w, alog_row, bias_row, onorm)


def _merge_kernel(x_ref, ya_ref, yb_ref, yc_ref, g0_ref, g1_ref, g2_ref, gb_ref,
                  woa_ref, wob_ref, woc_ref, wout_ref, o_ref, *, d):
    def branch(y_ref, w_ref, gate_ref, idx):
        gate = jax.nn.sigmoid(gate_ref[...].astype(F32) + gb_ref[:, idx * d:(idx + 1) * d])
        return gate * jnp.dot(y_ref[...], w_ref[...], preferred_element_type=F32)

    mixed = (branch(ya_ref, woa_ref, g0_ref, 0) + branch(yb_ref, wob_ref, g1_ref, 1)
             + branch(yc_ref, woc_ref, g2_ref, 2))
    o_ref[...] = x_ref[...] + jnp.dot(mixed.astype(BF16), wout_ref[...], preferred_element_type=F32)


def _merge(xf, ya, yb, yc, y, gate_bias, woa, wob, woc, wout, tm):
    m, d = xf.shape
    rowblk = lambda w, cb: pl.BlockSpec((tm, w), lambda i: (i, cb))
    return pl.pallas_call(
        functools.partial(_merge_kernel, d=d),
        grid=(m // tm,),
        in_specs=[
            rowblk(d, 0), rowblk(MIX_W, 0), rowblk(MIX_W, 0), rowblk(MIX_W, 0),
            rowblk(d, COL_GATES // d), rowblk(d, COL_GATES // d + 1), rowblk(d, COL_GATES // d + 2),
            _const_spec((1, 3 * d)),
            _const_spec((MIX_W, d)), _const_spec((MIX_W, d)), _const_spec((MIX_W, d)),
            _const_spec((d, d)),
        ],
        out_specs=rowblk(d, 0),
        out_shape=jax.ShapeDtypeStruct((m, d), F32),
        compiler_params=_cparams("parallel"),
        name="merge_out",
    )(xf, ya, yb, yc, y, y, y, gate_bias, woa, wob, woc, wout)


def _mem_kv_kernel(m_ref, g_ref, wkv_ref, kn_ref, k_ref, v_ref):
    mn = _rms(m_ref[...], g_ref[...]).astype(BF16)
    kv = jnp.dot(mn, wkv_ref[...], preferred_element_type=F32)
    for h in range(HEADS):
        sl = slice(h * HEAD_DIM, (h + 1) * HEAD_DIM)
        k_ref[:, sl] = _rms(kv[:, sl], kn_ref[...]).astype(BF16)
    v_ref[...] = kv[:, MIX_W:].astype(BF16)


def _mem_kv(memf, gain, wkv, knorm, batch, n_mem):
    d = memf.shape[1]
    return pl.pallas_call(
        _mem_kv_kernel,
        grid=(batch,),
        in_specs=[
            pl.BlockSpec((n_mem, d), lambda b: (b, 0)),
            _const_spec((1, d)),
            _const_spec((d, 2 * MIX_W)),
            _const_spec((1, HEAD_DIM)),
        ],
        out_specs=[pl.BlockSpec((n_mem, MIX_W), lambda b: (b, 0))] * 2,
        out_shape=[jax.ShapeDtypeStruct((batch * n_mem, MIX_W), BF16)] * 2,
        compiler_params=_cparams("parallel"),
        name="mem_kv",
    )(memf, gain, wkv, knorm)


def _mem_attn_kernel(x_ref, g_ref, wq_ref, qn_ref, k_ref, v_ref, wo_ref, o_ref):
    x = x_ref[...]
    hq = _rms(x, g_ref[...]).astype(BF16)
    q_all = jnp.dot(hq, wq_ref[...], preferred_element_type=F32)
    outs = []
    for h in range(HEADS):
        sl = slice(h * HEAD_DIM, (h + 1) * HEAD_DIM)
        q = (_rms(q_all[:, sl], qn_ref[...]) * (HEAD_DIM ** -0.5)).astype(BF16)
        s = lax.dot_general(q, k_ref[:, sl], NT_DIMS, preferred_element_type=F32)
        p = jnp.exp(s - jnp.max(s, axis=-1, keepdims=True))
        p = p / jnp.sum(p, axis=-1, keepdims=True)
        outs.append(jnp.dot(p.astype(BF16), v_ref[:, sl], preferred_element_type=F32).astype(BF16))
    o = jnp.concatenate(outs, axis=-1)
    o_ref[...] = x + jnp.dot(o, wo_ref[...], preferred_element_type=F32)


def _mem_attn(xf, gain, wq, qnorm, kmem, vmem, wo, batch, seq, n_mem, tm):
    m, d = xf.shape
    nt = seq // tm
    return pl.pallas_call(
        _mem_attn_kernel,
        grid=(batch, nt),
        in_specs=[
            pl.BlockSpec((tm, d), lambda b, i: (b * nt + i, 0)),
            _const_spec((1, d)),
            _const_spec((d, MIX_W)),
            _const_spec((1, HEAD_DIM)),
            pl.BlockSpec((n_mem, MIX_W), lambda b, i: (b, 0)),
            pl.BlockSpec((n_mem, MIX_W), lambda b, i: (b, 0)),
            _const_spec((MIX_W, d)),
        ],
        out_specs=pl.BlockSpec((tm, d), lambda b, i: (b * nt + i, 0)),
        out_shape=jax.ShapeDtypeStruct((m, d), F32),
        compiler_params=_cparams("parallel", "parallel"),
        name="mem_attn",
    )(xf, gain, wq, qnorm, kmem, vmem, wo)


def _ffn_kernel(x_ref, g_ref, wup_ref, cw_ref, cb_ref, wdn_ref, o_ref, carry_ref, xs_ref,
                *, tm, d_ff, n_chunk):
    @pl.when(pl.program_id(1) == 0)
    def _():
        carry_ref[...] = jnp.zeros_like(carry_ref)

    x = x_ref[...]
    h = _rms(x, g_ref[...]).astype(BF16)

    def conv_half(c0):
        u = jnp.dot(h, wup_ref[:, c0:c0 + n_chunk], preferred_element_type=F32)
        xs_ref[0:SUBLANES, :] = carry_ref[:, c0:c0 + n_chunk]
        xs_ref[SUBLANES:SUBLANES + tm, :] = u
        carry_ref[:, c0:c0 + n_chunk] = xs_ref[tm:tm + SUBLANES, :]
        out = jnp.broadcast_to(cb_ref[:, c0:c0 + n_chunk], (tm, n_chunk))
        for tap in range(FFN_CONV):
            off = SUBLANES - (FFN_CONV - 1) + tap
            out = out + cw_ref[tap:tap + 1, c0:c0 + n_chunk] * xs_ref[off:off + tm, :]
        return out

    acc = x
    for c0 in range(0, d_ff, n_chunk):
        a = conv_half(c0)
        b = conv_half(d_ff + c0)
        act = (_silu(a) * b).astype(BF16)
        acc = acc + jnp.dot(act, wdn_ref[c0:c0 + n_chunk, :], preferred_element_type=F32)
    o_ref[...] = acc


def _ffn(xf, gain, wup, conv_w, conv_b, wdn, batch, seq, tm, n_chunk):
    m, d = xf.shape
    d_ff = wdn.shape[0]
    nt = seq // tm
    return pl.pallas_call(
        functools.partial(_ffn_kernel, tm=tm, d_ff=d_ff, n_chunk=n_chunk),
        grid=(batch, nt),
        in_specs=[
            pl.BlockSpec((tm, d), lambda b, i: (b * nt + i, 0)),
            _const_spec((1, d)),
            _const_spec((d, 2 * d_ff)),
            _const_spec((FFN_CONV, 2 * d_ff)),
            _const_spec((1, 2 * d_ff)),
            _const_spec((d_ff, d)),
        ],
        out_specs=pl.BlockSpec((tm, d), lambda b, i: (b * nt + i, 0)),
        out_shape=jax.ShapeDtypeStruct((m, d), F32),
        scratch_shapes=[pltpu.VMEM((SUBLANES, 2 * d_ff), F32),
                        pltpu.VMEM((tm + SUBLANES, n_chunk), F32)],
        compiler_params=_cparams("parallel", "arbitrary"),
        name="conv_ffn",
    )(xf, gain, wup, conv_w, conv_b, wdn)


def _lane_row(values, lane0):
    return jnp.zeros((1, LANES), F32).at[0, lane0:lane0 + values.shape[0]].set(values.astype(F32))


def _pack_w_in(w):
    splits = (512, 512, 512, 4, 512, 512, 512, 4, 4, 512, 512, 512, 512)
    parts, off = [], 0
    for s in splits:
        parts.append(w[:, off:off + s])
        off += s
    gates = w[:, off:]
    fq, fk, fv, ff, gq, gk, gv, gb, ga, gz, sq, sk, sv = parts
    sq = sq * (HEAD_DIM ** -0.5)
    big =jnp.concatenate([fq, fk, fv, gq, gk, gv, gz, sq, sk, sv, gates], axis=1).astype(BF16)
    small = jnp.concatenate(
        [ff, gb, ga, jnp.zeros((w.shape[0], LANES - 3 * HEADS), w.dtype)], axis=1).astype(BF16)
    return big, small


def _tile(n, pref):
    return pref if n % pref == 0 else n


def kernel(x, mem, norm_mix, w_in, fox_fbias, fox_qnorm, fox_knorm, gdn_conv, gdn_a_log, gdn_dt_bias, gdn_onorm, gate_bias, w_oa, w_ob, w_oc, w_out, norm_xq, norm_mem, w_mq, w_mkv, mq_norm, mk_norm, w_mo, norm_ffn, w_up, ffn_conv, ffn_conv_b, w_down):
    batch, seq, d = x.shape
    n_mem = mem.shape[1]
    depth = w_in.shape[0]
    d_ff = w_down.shape[1]
    m = batch * seq
    assert d == 1024 and w_in.shape[2] == N_BIG + 3 * HEADS and seq % GDN_CHUNK == 0

    tm = _tile(seq, 512)
    tq = _tile(seq, 512)
    tk_sb = _tile(tq, 256)
    t_prep = _tile(seq, 256)
    tg = _tile(seq, 256)
    n_chunk = d_ff

    xf = x.reshape(m, d)
    memf = mem.reshape(batch * n_mem, d)
    row = lambda v: v.reshape(1, -1).astype(F32)
    for l in range(depth):
        w_big, w_small = _pack_w_in(w_in[l])
        y, g = _inproj(xf, row(norm_mix[l]), w_big, w_small, tm)

        q_aug, k_aug, c_ends = _fox_prep(y, g, _lane_row(fox_fbias[l], LANE_FF), row(fox_qnorm[l]),
                                         row(fox_knorm[l]), batch, seq, t_prep)
        step = tq // t_prep
        cend = c_ends[:, 0, LANE_FF:LANE_FF + HEADS].reshape(batch, seq // t_prep, HEADS)
        cend = cend[:, step - 1::step, :].transpose(0, 2, 1).reshape(batch * HEADS, seq // tq)
        ya = _fox_attn(cend, q_aug, k_aug, y, batch, seq, tq)
        yb = _gdn(y, g, gdn_conv[l].astype(F32), _lane_row(gdn_a_log[l], LANE_GA),
                  _lane_row(gdn_dt_bias[l], LANE_GA), row(gdn_onorm[l]), batch, seq, tg)
        yc = _sb_attn(y, batch, seq, tq, tk_sb)

        xf = _merge(xf, ya, yb, yc, y, row(gate_bias[l]), w_oa[l].astype(BF16),
                    w_ob[l].astype(BF16), w_oc[l].astype(BF16), w_out[l].astype(BF16), tm)

        kmem, vmem = _mem_kv(memf, row(norm_mem[l]), w_mkv[l].astype(BF16), row(mk_norm[l]),
                             batch, n_mem)
        xf = _mem_attn(xf, row(norm_xq[l]), w_mq[l].astype(BF16), row(mq_norm[l]), kmem, vmem,
                       w_mo[l].astype(BF16), batch, seq, n_mem, tm)

        xf = _ffn(xf, row(norm_ffn[l]), w_up[l].astype(BF16), ffn_conv[l].astype(F32),
                  row(ffn_conv_b[l]), w_down[l].astype(BF16), batch, seq, tm, n_chunk)
    return xf.reshape(batch, seq, d)
```

```python
import functools

import jax
import jax.numpy as jnp
from jax import lax
from jax.experimental import pallas as pl
from jax.experimental.pallas import tpu as pltpu

F32 = jnp.float32
BF16 = jnp.bfloat16
EPS = 1e-6
LOG2E = 1.4426950408889634
SKIP_EXP = -110.0
SKIP_EXP2 = -155.0
NORM_SLACK = 1.01

HEADS = 4
HEAD_DIM = 128
MIX_W = HEADS * HEAD_DIM
GDN_CHUNK = 64
GDN_CONV = 4
FFN_CONV = 3
LANES = 128
SUBLANES = 8
VMEM_LIMIT = 56 * 1024 * 1024

COL_FQ, COL_FK, COL_FV = 0, 512, 1024
COL_GQ, COL_GK, COL_GV, COL_GZ = 1536, 2048, 2560, 3072
COL_SQ, COL_SK, COL_SV = 3584, 4096, 4608
COL_GATES = 5120
N_BIG = 8192
LANE_FF, LANE_GB, LANE_GA = 0, 4, 8

NT_DIMS = (((1,), (1,)), ((), ()))
TN_DIMS = (((0,), (0,)), ((), ()))


def _cparams(*sem):
    return pltpu.CompilerParams(dimension_semantics=sem, vmem_limit_bytes=VMEM_LIMIT)


def _const_spec(shape):
    nd = len(shape)
    return pl.BlockSpec(shape, lambda *_: (0,) * nd, pipeline_mode=pl.Buffered(1))


def _rms(xf, gain):
    return xf * lax.rsqrt(jnp.mean(xf * xf, axis=-1, keepdims=True) + EPS) * gain


def _softplus(x):
    return jnp.maximum(x, 0.0) + jnp.log1p(jnp.exp(-jnp.abs(x)))


def _silu(x):
    h = 0.5 * x
    return h + h * jnp.tanh(h)


def _split3(x):
    a = x.astype(BF16)
    r = x - a.astype(F32)
    b = r.astype(BF16)
    c = (r - b.astype(F32)).astype(BF16)
    return a, b, c


def _lane_col(x, lane):
    idx = lax.broadcasted_iota(jnp.int32, x.shape, 1)
    return jnp.sum(jnp.where(idx == lane, x, 0.0), axis=-1, keepdims=True)


def _inproj_kernel(x_ref, g_ref, w_ref, ws_ref, y_ref, gs_ref, *, n_chunk):
    h = _rms(x_ref[...], g_ref[...]).astype(BF16)
    for c in range(0, N_BIG, n_chunk):
        y_ref[:, c:c + n_chunk] = jnp.dot(
            h, w_ref[:, c:c + n_chunk], preferred_element_type=F32).astype(BF16)
    gs_ref[...] = jnp.dot(h, ws_ref[...], preferred_element_type=F32)


def _inproj(xf, gain, w_big, w_small, tm):
    m, d = xf.shape
    return pl.pallas_call(
        functools.partial(_inproj_kernel, n_chunk=512),
        grid=(m // tm,),
        in_specs=[
            pl.BlockSpec((tm, d), lambda i: (i, 0)),
            _const_spec((1, d)),
            _const_spec((d, N_BIG)),
            _const_spec((d, LANES)),
        ],
        out_specs=[
            pl.BlockSpec((tm, N_BIG), lambda i: (i, 0)),
            pl.BlockSpec((tm, LANES), lambda i: (i, 0)),
        ],
        out_shape=[jax.ShapeDtypeStruct((m, N_BIG), BF16),
                   jax.ShapeDtypeStruct((m, LANES), F32)],
        compiler_params=_cparams("parallel"),
        name="inproj",
    )(xf, gain, w_big, w_small)


def _fox_prep_kernel(yq_ref, yk_ref, g_ref, fb_ref, qn_ref, kn_ref, qa_ref, ka_ref, ce_ref,
                     carry_ref, *, t):
    @pl.when(pl.program_id(1) == 0)
    def _():
        carry_ref[...] = jnp.zeros_like(carry_ref)

    lane = lax.broadcasted_iota(jnp.int32, (t, LANES), 1)
    x = g_ref[...] + fb_ref[...]
    logf = jnp.minimum(x, 0.0) - jnp.log1p(jnp.exp(-jnp.abs(x)))
    logf = jnp.where(lane < LANE_FF + HEADS, logf, 0.0)
    row = lax.broadcasted_iota(jnp.int32, (t, t), 0)
    col = lax.broadcasted_iota(jnp.int32, (t, t), 1)
    tri = jnp.where(row >= col, 1.0, 0.0).astype(BF16)
    l1, l2, l3 = _split3(logf)
    c_all = (jnp.dot(tri, l1, preferred_element_type=F32)
             + jnp.dot(tri, l2, preferred_element_type=F32)
             + jnp.dot(tri, l3, preferred_element_type=F32)) + carry_ref[0:1, :]
    carry_ref[0:1, :] = c_all[t - 1:t, :]
    ce_ref[0] = jnp.broadcast_to(c_all[t - 1:t, :] * LOG2E, (SUBLANES, LANES))

    scale = HEAD_DIM ** -0.5 * LOG2E
    for h in range(HEADS):
        sl = slice(h * HEAD_DIM, (h + 1) * HEAD_DIM)
        qn = _rms(yq_ref[:, sl].astype(F32), qn_ref[...]) * scale
        kn = _rms(yk_ref[:, sl].astype(F32), kn_ref[...])
        cb = jnp.broadcast_to(_lane_col(c_all, LANE_FF + h) * LOG2E, (t, LANES))
        c1, c2, c3 = (p.astype(F32) for p in _split3(cb))
        aux_q = jnp.where(lane == 0, c1, jnp.where(lane == 1, c2, jnp.where(
            lane == 2, c3, jnp.where(lane < 6, 1.0, 0.0))))
        aux_k = jnp.where(lane < 3, 1.0, jnp.where(lane == 3, -c1, jnp.where(
            lane == 4, -c2, jnp.where(lane == 5, -c3, 0.0))))
        base = 2 * h * HEAD_DIM
        qa_ref[:, base:base + HEAD_DIM] = qn.astype(BF16)
        qa_ref[:, base + HEAD_DIM:base + 2 * HEAD_DIM] = aux_q.astype(BF16)
        ka_ref[:, base:base + HEAD_DIM] = kn.astype(BF16)
        ka_ref[:, base + HEAD_DIM:base + 2 * HEAD_DIM] = aux_k.astype(BF16)


def _fox_prep(y, g, fb_row, qn, kn, batch, seq, t):
    m = batch * seq
    nt = seq // t
    rows = lambda b, i: b * nt + i
    return pl.pallas_call(
        functools.partial(_fox_prep_kernel, t=t),
        grid=(batch, nt),
        in_specs=[
            pl.BlockSpec((t, MIX_W), lambda b, i: (rows(b, i), COL_FQ // MIX_W)),
            pl.BlockSpec((t, MIX_W), lambda b, i: (rows(b, i), COL_FK // MIX_W)),
            pl.BlockSpec((t, LANES), lambda b, i: (rows(b, i), 0)),
            _const_spec((1, LANES)),
            _const_spec((1, HEAD_DIM)),
            _const_spec((1, HEAD_DIM)),
        ],
        out_specs=[
            pl.BlockSpec((t, 2 * MIX_W), lambda b, i: (rows(b, i), 0)),
            pl.BlockSpec((t, 2 * MIX_W), lambda b, i: (rows(b, i), 0)),
            pl.BlockSpec((1, SUBLANES, LANES), lambda b, i: (rows(b, i), 0, 0)),
        ],
        out_shape=[jax.ShapeDtypeStruct((m, 2 * MIX_W), BF16)] * 2
        + [jax.ShapeDtypeStruct((batch * nt, SUBLANES, LANES), F32)],
        scratch_shapes=[pltpu.VMEM((SUBLANES, LANES), F32)],
        compiler_params=_cparams("parallel", "arbitrary"),
        name="fox_prep",
    )(y, y, g, fb_row, qn, kn)


def _bounded_sweep(i, stage_a, stage_b, first_tile_fn):
    stage_a(i, 0, True)

    @pl.when(i == 0)
    def _():
        stage_b(i, 0, True)

    @pl.when(i > 0)
    def _():
        stage_a(i - 1, 1, False)
        stage_b(i, 0, True)
        lo = first_tile_fn()
        n = i - lo
        n_pairs = (n - 1) // 2

        def body(p, carry):
            t = i - 1 - 2 * p
            stage_a(t - 1, 0, False)
            stage_b(t, 1, False)
            stage_a(t - 2, 1, False)
            stage_b(t - 1, 0, False)
            return carry

        lax.fori_loop(0, n_pairs, body, 0)
        t = i - 1 - 2 * n_pairs

        @pl.when(jnp.logical_and(n >= 1, t == lo))
        def _():
            stage_b(t, 1, False)

        @pl.when(jnp.logical_and(n >= 1, t == lo + 1))
        def _():
            stage_a(lo, 0, False)
            stage_b(t, 1, False)
            stage_b(lo, 0, False)


def _checked_sweep(i, stage_a, stage_b, last_fn):
    stage_a(i, 0, True)

    @pl.when(i == 0)
    def _():
        stage_b(i, 0, True)

    @pl.when(i > 0)
    def _():
        last_0 = last_fn(0)

        @pl.when(last_0)
        def _():
            stage_b(i, 0, True)

        @pl.when(jnp.logical_not(last_0))
        def _():
            stage_a(i - 1, 1, False)
            stage_b(i, 0, True)

            def body(carry):
                t = carry[0]
                last_1 = jnp.logical_or(t == 0, last_fn(1))

                @pl.when(last_1)
                def _():
                    stage_b(t, 1, False)

                @pl.when(jnp.logical_not(last_1))
                def _():
                    stage_a(t - 1, 0, False)
                    stage_b(t, 1, False)

                last_2 = jnp.logical_or(t == 1, last_fn(0))
                more = jnp.logical_not(last_1)

                @pl.when(jnp.logical_and(more, last_2))
                def _():
                    stage_b(t - 1, 0, False)

                @pl.when(jnp.logical_and(more, jnp.logical_not(last_2)))
                def _():
                    stage_a(t - 2, 1, False)
                    stage_b(t - 1, 0, False)

                return t - 2, jnp.logical_or(last_1, last_2)

            lax.while_loop(lambda carry: jnp.logical_not(carry[1]), body, (i - 1, last_0))


def _max_sq_norm(ref, tq, seq, width):
    def chunk(c, best):
        x = ref[pl.ds(pl.multiple_of(c * tq, tq), tq), 0:width].astype(F32)
        return jnp.maximum(best, jnp.sum(x * x, axis=-1, keepdims=True))

    best = lax.fori_loop(0, seq // tq, chunk, jnp.zeros((tq, 1), F32))
    return jnp.max(best, axis=0, keepdims=True)


def _fox_attn_kernel(cend_ref, q_ref, k_ref, v_ref, o_ref, s_ref, m_ref, l_ref, acc_ref, zb_ref,
                     kmax_ref, *, tq, seq):
    @pl.when(pl.program_id(2) == 0)
    def _():
        kmax_ref[...] = jnp.broadcast_to(_max_sq_norm(k_ref, tq, seq, HEAD_DIM), kmax_ref.shape)

    m_ref[...] = jnp.full((tq, LANES), -jnp.inf, F32)
    l_ref[...] = jnp.zeros((tq, LANES), F32)
    acc_ref[...] = jnp.zeros((tq, HEAD_DIM), F32)

    qf = q_ref[:, 0:HEAD_DIM].astype(F32)
    aux = q_ref[:, HEAD_DIM:2 * HEAD_DIM].astype(F32)
    lane = lax.broadcasted_iota(jnp.int32, (tq, LANES), 1)
    c_q = jnp.sum(jnp.where(lane < 3, aux, 0.0), axis=-1, keepdims=True)
    q_sq = jnp.sum(qf * qf, axis=-1, keepdims=True)
    zb_ref[...] = jnp.sqrt(q_sq * kmax_ref[0:1, 0:1]) * NORM_SLACK + c_q

    def scores(j, slot, masked):
        del masked
        k = k_ref[pl.ds(pl.multiple_of(j * tq, tq), tq), :]
        s_ref[slot] = lax.dot_general(q_ref[...], k, NT_DIMS, preferred_element_type=F32)

    def softmax_pv(j, slot, masked):
        v = v_ref[pl.ds(pl.multiple_of(j * tq, tq), tq), :]
        s = s_ref[slot]
        if masked:
            row = lax.broadcasted_iota(jnp.int32, (tq, tq), 0)
            col = lax.broadcasted_iota(jnp.int32, (tq, tq), 1)
            s = jnp.where(col <= row, s, -jnp.inf)
        m_prev = m_ref[...]
        m_new = jnp.maximum(m_prev, jnp.max(s, axis=-1, keepdims=True))
        p = jnp.exp2(s - jnp.concatenate([m_new] * (tq // LANES), axis=1))
        alpha = jnp.exp2(m_prev - m_new)
        l_ref[...] = alpha * l_ref[...] + jnp.sum(p, axis=-1, keepdims=True)
        acc_ref[...] = alpha * acc_ref[...] + jnp.dot(p.astype(BF16), v, preferred_element_type=F32)
        m_ref[...] = m_new

    def first_tile_fn():
        slack = jnp.max(zb_ref[...] - m_ref[...])
        row = pl.program_id(0) * HEADS + pl.program_id(1)

        def count(j, lo):
            return lo + jnp.where(slack - cend_ref[row, j] < SKIP_EXP2, 1, 0)

        return lax.fori_loop(0, pl.program_id(2), count, 0)

    _bounded_sweep(pl.program_id(2), scores, softmax_pv, first_tile_fn)
    o_ref[...] = (acc_ref[...] / l_ref[...]).astype(BF16)


def _fox_attn(cend, q_aug, k_aug, y, batch, seq, tq):
    m = batch * seq
    nq = seq // tq
    return pl.pallas_call(
        functools.partial(_fox_attn_kernel, tq=tq, seq=seq),
        grid=(batch, HEADS, nq),
        in_specs=[
            pl.BlockSpec(memory_space=pltpu.SMEM),
            pl.BlockSpec((tq, 2 * HEAD_DIM), lambda b, h, i: (b * nq + i, h)),
            pl.BlockSpec((seq, 2 * HEAD_DIM), lambda b, h, i: (b, h)),
            pl.BlockSpec((seq, HEAD_DIM), lambda b, h, i: (b, COL_FV // HEAD_DIM + h)),
        ],
        out_specs=pl.BlockSpec((tq, HEAD_DIM), lambda b, h, i: (b * nq + i, h)),
        out_shape=jax.ShapeDtypeStruct((m, MIX_W), BF16),
        scratch_shapes=[pltpu.VMEM((2, tq, tq), F32),
                        pltpu.VMEM((tq, LANES), F32), pltpu.VMEM((tq, LANES), F32),
                        pltpu.VMEM((tq, HEAD_DIM), F32),
                        pltpu.VMEM((tq, 1), F32), pltpu.VMEM((SUBLANES, LANES), F32)],
        compiler_params=_cparams("parallel", "parallel", "arbitrary"),
        name="fox_attn",
    )(cend, q_aug, k_aug, y)


def _sb_attn_kernel(q_ref, k_ref, v_ref, o_ref, zw_ref, tot_ref, after_ref, acc_ref, zb_ref,
                    kmax_ref, *, tq, tk, seq):
    @pl.when(pl.program_id(2) == 0)
    def _():
        kmax_ref[...] = jnp.broadcast_to(_max_sq_norm(k_ref, tq, seq, HEAD_DIM), kmax_ref.shape)

    qf = q_ref[...].astype(F32)
    zb_ref[...] = jnp.sqrt(jnp.sum(qf * qf, axis=-1, keepdims=True) * kmax_ref[0:1, 0:1]) * NORM_SLACK

    n_sub = tq // tk
    row_k = lax.broadcasted_iota(jnp.int32, (tk, tk), 0)
    col_k = lax.broadcasted_iota(jnp.int32, (tk, tk), 1)
    neg_suffix = jnp.where(row_k >= col_k, -1.0, 0.0).astype(BF16)
    after_ref[...] = jnp.zeros((tq, 1), F32)
    acc_ref[...] = jnp.zeros((tq, HEAD_DIM), F32)

    def valid_mask(sub):
        row = lax.broadcasted_iota(jnp.int32, (tq, tk), 0)
        col = lax.broadcasted_iota(jnp.int32, (tq, tk), 1) + sub * tk
        return col < row

    def logits(j, slot, masked):
        for sub in range(n_sub):
            k = k_ref[pl.ds(pl.multiple_of(j * tq + sub * tk, tk), tk), :]
            z = lax.dot_general(q_ref[...], k, NT_DIMS, preferred_element_type=F32)
            sp = jnp.maximum(z, 0.0) + jnp.log(1.0 + jnp.exp2(jnp.abs(z) * (-LOG2E)))
            if masked:
                sp = jnp.where(valid_mask(sub), sp, 0.0)
            within = jnp.dot(sp.astype(BF16), neg_suffix, preferred_element_type=F32)
            zw_ref[slot, sub] = z + within
            tot_ref[slot, sub] = within[:, 0:1]

    def weights_pv(j, slot, masked):
        for sub in reversed(range(n_sub)):
            v = v_ref[pl.ds(pl.multiple_of(j * tq + sub * tk, tk), tk), :]
            after = after_ref[...]
            e = zw_ref[slot, sub] + after
            if masked:
                e = jnp.where(valid_mask(sub), e, -jnp.inf)
            a = jnp.exp(e)
            acc_ref[...] += jnp.dot(a.astype(BF16), v, preferred_element_type=F32)
            after_ref[...] = after + tot_ref[slot, sub]

    def last_fn(slot):
        after = after_ref[...]
        for sub in reversed(range(n_sub)):
            after = after + tot_ref[slot, sub]
        return jnp.max(zb_ref[...] + after) < SKIP_EXP

    _checked_sweep(pl.program_id(2), logits, weights_pv, last_fn)
    o_ref[...] = acc_ref[...].astype(BF16)


def _sb_attn(y, batch, seq, tq, tk):
    m = batch * seq
    nq = seq // tq
    return pl.pallas_call(
        functools.partial(_sb_attn_kernel, tq=tq, tk=tk, seq=seq),
        grid=(batch, HEADS, nq),
        in_specs=[
            pl.BlockSpec((tq, HEAD_DIM), lambda b, h, i: (b * nq + i, COL_SQ // HEAD_DIM + h)),
            pl.BlockSpec((seq, HEAD_DIM), lambda b, h, i: (b, COL_SK // HEAD_DIM + h)),
            pl.BlockSpec((seq, HEAD_DIM), lambda b, h, i: (b, COL_SV // HEAD_DIM + h)),
        ],
        out_specs=pl.BlockSpec((tq, HEAD_DIM), lambda b, h, i: (b * nq + i, h)),
        out_shape=jax.ShapeDtypeStruct((m, MIX_W), BF16),
        scratch_shapes=[pltpu.VMEM((2, tq // tk, tq, tk), F32),
                        pltpu.VMEM((2, tq // tk, tq, 1), F32),
                        pltpu.VMEM((tq, 1), F32), pltpu.VMEM((tq, HEAD_DIM), F32),
                        pltpu.VMEM((tq, 1), F32), pltpu.VMEM((SUBLANES, LANES), F32)],
        compiler_params=_cparams("parallel", "parallel", "arbitrary"),
        name="sb_attn",
    )(y, y, y)


def _bdot(a, b):
    return jnp.dot(a.astype(BF16), b.astype(BF16), preferred_element_type=F32)


def _unit_lower_inverses(a_list, eye):
    c = eye.shape[0]
    xs = [eye - a for a in a_list]
    ps = [_bdot(a, a) for a in a_list]
    power = 2
    while 2 * power < c:
        xps = [_bdot(jnp.concatenate([x, p], axis=0), p) for x, p in zip(xs, ps)]
        xs = [x + xp[:c] for x, xp in zip(xs, xps)]
        ps = [xp[c:] for xp in xps]
        power *= 2
    return [x + _bdot(x, p) for x, p in zip(xs, ps)]


def _gdn_kernel(yq_ref, yk_ref, yv_ref, yz_ref, g_ref, cw_ref, alog_ref, bias_ref, on_ref,
                o_ref, state_ref, xs_ref, *, tg):
    @pl.when(pl.program_id(1) == 0)
    def _():
        state_ref[...] = jnp.zeros_like(state_ref)
        xs_ref[0:SUBLANES, :] = jnp.zeros((SUBLANES, 3 * MIX_W), F32)

    c = GDN_CHUNK
    xs_ref[SUBLANES:SUBLANES + tg, 0:MIX_W] = yq_ref[...].astype(F32)
    xs_ref[SUBLANES:SUBLANES + tg, MIX_W:2 * MIX_W] = yk_ref[...].astype(F32)
    xs_ref[SUBLANES:SUBLANES + tg, 2 * MIX_W:3 * MIX_W] = yv_ref[...].astype(F32)
    conv = jnp.zeros((tg, 3 * MIX_W), F32)
    for tap in range(GDN_CONV):
        off = SUBLANES - (GDN_CONV - 1) + tap
        conv = conv + cw_ref[tap:tap + 1, :] * xs_ref[off:off + tg, :]
    xs_ref[0:SUBLANES, :] = xs_ref[tg:tg + SUBLANES, :]
    qkv = _silu(conv)

    g_all = g_ref[...]
    beta_all = jax.nn.sigmoid(g_all)
    glog_all = -jnp.exp(alog_ref[...]) * _softplus(g_all + bias_ref[...])
    lane = lax.broadcasted_iota(jnp.int32, (tg, LANES), 1)
    glog_all = jnp.where((lane >= LANE_GA) & (lane < LANE_GA + HEADS), glog_all, 0.0)

    row = lax.broadcasted_iota(jnp.int32, (c, c), 0)
    col = lax.broadcasted_iota(jnp.int32, (c, c), 1)
    causal = row >= col
    strict = row > col
    eye = jnp.where(row == col, 1.0, 0.0)
    tril = jnp.where(causal, 1.0, 0.0).astype(BF16)

    items = [(n, h) for n in range(tg // c) for h in range(HEADS)]
    gc_alls, gc_rows = [], []
    for n in range(tg // c):
        g1, g2, g3 = _split3(glog_all[n * c:(n + 1) * c, :])
        gc_all = (jnp.dot(tril, g1, preferred_element_type=F32)
                  + jnp.dot(tril, g2, preferred_element_type=F32)
                  + jnp.dot(tril, g3, preferred_element_type=F32))
        gc_alls.append(gc_all)
        gc_rows.append(gc_all.T)

    pre = []
    for n, h in items:
        r0 = n * c
        cq = qkv[r0:r0 + c, h * HEAD_DIM:(h + 1) * HEAD_DIM]
        ck = qkv[r0:r0 + c, MIX_W + h * HEAD_DIM:MIX_W + (h + 1) * HEAD_DIM]
        v = qkv[r0:r0 + c, 2 * MIX_W + h * HEAD_DIM:2 * MIX_W + (h + 1) * HEAD_DIM]
        q = cq * lax.rsqrt(jnp.sum(cq * cq, axis=-1, keepdims=True) + EPS) * (HEAD_DIM ** -0.5)
        k = ck * lax.rsqrt(jnp.sum(ck * ck, axis=-1, keepdims=True) + EPS)
        beta = _lane_col(beta_all[r0:r0 + c, :], LANE_GB + h)
        gc = _lane_col(gc_alls[n], LANE_GA + h)
        gc_row = gc_rows[n][LANE_GA + h:LANE_GA + h + 1, :]
        g_last = gc[c - 1:c, :]
        decay = jnp.exp(jnp.where(causal, gc - gc_row, -jnp.inf))
        kb = k.astype(BF16)
        kk = lax.dot_general(kb, kb, NT_DIMS, preferred_element_type=F32)
        qk = lax.dot_general(q.astype(BF16), kb, NT_DIMS, preferred_element_type=F32)
        pre.append(dict(
            a_strict=jnp.where(strict, beta * kk * decay, 0.0),
            attn=jnp.where(causal, qk * decay, 0.0).astype(BF16),
            rhs=jnp.concatenate([v * beta, k * (beta * jnp.exp(gc))], axis=-1),
            q_dec=(q * jnp.exp(gc)).astype(BF16),
            k_tail=(k * jnp.exp(g_last - gc)).astype(BF16),
            s_dec=jnp.exp(g_last)))

    t_invs = _unit_lower_inverses([p["a_strict"] for p in pre], eye)
    sols = [_bdot(t, p["rhs"]) for t, p in zip(t_invs, pre)]

    for n in range(tg // c):
        r0 = n * c
        idx = [n * HEADS + h for h in range(HEADS)]
        states = [state_ref[h] for h in range(HEADS)]
        sbs = [s.astype(BF16) for s in states]
        vnbs = [(sols[i][:, :HEAD_DIM] - jnp.dot(sols[i][:, HEAD_DIM:].astype(BF16), sb,
                                                 preferred_element_type=F32)).astype(BF16)
                for i, sb in zip(idx, sbs)]
        outs = [jnp.dot(pre[i]["q_dec"], sb, preferred_element_type=F32)
                + jnp.dot(pre[i]["attn"], vnb, preferred_element_type=F32)
                for i, sb, vnb in zip(idx, sbs, vnbs)]
        for h in range(HEADS):
            i = idx[h]
            state_ref[h] = states[h] * pre[i]["s_dec"] + lax.dot_general(
                pre[i]["k_tail"], vnbs[h], TN_DIMS, preferred_element_type=F32)
            sl = slice(h * HEAD_DIM, (h + 1) * HEAD_DIM)
            z = yz_ref[r0:r0 + c, sl].astype(F32)
            o_ref[r0:r0 + c, sl] = (_rms(outs[h], on_ref[...]) * _silu(z)).astype(BF16)


def _gdn(y, g, conv_w, alog_row, bias_row, onorm, batch, seq, tg):
    m = batch * seq
    nt = seq // tg
    rows = lambda b, i: b * nt + i
    ycol = lambda off: pl.BlockSpec((tg, MIX_W), lambda b, i: (rows(b, i), off // MIX_W))
    return pl.pallas_call(
        functools.partial(_gdn_kernel, tg=tg),
        grid=(batch, nt),
        in_specs=[
            ycol(COL_GQ), ycol(COL_GK), ycol(COL_GV), ycol(COL_GZ),
            pl.BlockSpec((tg, LANES), lambda b, i: (rows(b, i), 0)),
            _const_spec((GDN_CONV, 3 * MIX_W)),
            _const_spec((1, LANES)),
            _const_spec((1, LANES)),
            _const_spec((1, HEAD_DIM)),
        ],
        out_specs=pl.BlockSpec((tg, MIX_W), lambda b, i: (rows(b, i), 0)),
        out_shape=jax.ShapeDtypeStruct((m, MIX_W), BF16),
        scratch_shapes=[pltpu.VMEM((HEADS, HEAD_DIM, HEAD_DIM), F32),
                        pltpu.VMEM((tg + SUBLANES, 3 * MIX_W), F32)],
        compiler_params=_cparams("parallel", "arbitrary"),
        name="gdn",
    )(y, y, y, y, g, conv_w, alog_row, bias_row, onorm)


def _merge_mem_kernel(x_ref, ya_ref, yb_ref, yc_ref, g0_ref, g1_ref, g2_ref, gb_ref,
                      woa_ref, wob_ref, woc_ref, wout_ref,
                      gq_ref, wq_ref, qn_ref, k_ref, v_ref, wo_ref, o_ref, *, d):
    def branch(y_ref, w_ref, gate_ref, idx):
        gate = jax.nn.sigmoid(gate_ref[...].astype(F32) + gb_ref[:, idx * d:(idx + 1) * d])
        return gate * jnp.dot(y_ref[...], w_ref[...], preferred_element_type=F32)

    mixed = (branch(ya_ref, woa_ref, g0_ref, 0) + branch(yb_ref, wob_ref, g1_ref, 1)
             + branch(yc_ref, woc_ref, g2_ref, 2))
    x = x_ref[...] + jnp.dot(mixed.astype(BF16), wout_ref[...], preferred_element_type=F32)

    hq = _rms(x, gq_ref[...]).astype(BF16)
    q_all = jnp.dot(hq, wq_ref[...], preferred_element_type=F32)
    outs = []
    for h in range(HEADS):
        sl = slice(h * HEAD_DIM, (h + 1) * HEAD_DIM)
        q = (_rms(q_all[:, sl], qn_ref[...]) * (HEAD_DIM ** -0.5)).astype(BF16)
        s = lax.dot_general(q, k_ref[:, sl], NT_DIMS, preferred_element_type=F32)
        p = jnp.exp(s - jnp.max(s, axis=-1, keepdims=True))
        p = p / jnp.sum(p, axis=-1, keepdims=True)
        outs.append(jnp.dot(p.astype(BF16), v_ref[:, sl], preferred_element_type=F32).astype(BF16))
    o = jnp.concatenate(outs, axis=-1)
    o_ref[...] = x + jnp.dot(o, wo_ref[...], preferred_element_type=F32)


def _merge_mem(xf, ya, yb, yc, y, gate_bias, woa, wob, woc, wout, gain_q, wq, qnorm, kmem, vmem,
               wo, batch, seq, n_mem, tm):
    m, d = xf.shape
    nt = seq // tm
    rowblk = lambda w, cb: pl.BlockSpec((tm, w), lambda b, i: (b * nt + i, cb))
    return pl.pallas_call(
        functools.partial(_merge_mem_kernel, d=d),
        grid=(batch, nt),
        in_specs=[
            rowblk(d, 0), rowblk(MIX_W, 0), rowblk(MIX_W, 0), rowblk(MIX_W, 0),
            rowblk(d, COL_GATES // d), rowblk(d, COL_GATES // d + 1), rowblk(d, COL_GATES // d + 2),
            _const_spec((1, 3 * d)),
            _const_spec((MIX_W, d)), _const_spec((MIX_W, d)), _const_spec((MIX_W, d)),
            _const_spec((d, d)),
            _const_spec((1, d)),
            _const_spec((d, MIX_W)),
            _const_spec((1, HEAD_DIM)),
            pl.BlockSpec((n_mem, MIX_W), lambda b, i: (b, 0)),
            pl.BlockSpec((n_mem, MIX_W), lambda b, i: (b, 0)),
            _const_spec((MIX_W, d)),
        ],
        out_specs=rowblk(d, 0),
        out_shape=jax.ShapeDtypeStruct((m, d), F32),
        compiler_params=_cparams("parallel", "parallel"),
        name="merge_mem",
    )(xf, ya, yb, yc, y, y, y, gate_bias, woa, wob, woc, wout, gain_q, wq, qnorm, kmem, vmem, wo)


def _mem_kv_kernel(m_ref, g_ref, wkv_ref, kn_ref, k_ref, v_ref):
    mn = _rms(m_ref[...], g_ref[...]).astype(BF16)
    kv = jnp.dot(mn, wkv_ref[...], preferred_element_type=F32)
    for h in range(HEADS):
        sl = slice(h * HEAD_DIM, (h + 1) * HEAD_DIM)
        k_ref[:, sl] = _rms(kv[:, sl], kn_ref[...]).astype(BF16)
    v_ref[...] = kv[:, MIX_W:].astype(BF16)


def _mem_kv(memf, gain, wkv, knorm, batch, n_mem):
    d = memf.shape[1]
    return pl.pallas_call(
        _mem_kv_kernel,
        grid=(batch,),
        in_specs=[
            pl.BlockSpec((n_mem, d), lambda b: (b, 0)),
            _const_spec((1, d)),
            _const_spec((d, 2 * MIX_W)),
            _const_spec((1, HEAD_DIM)),
        ],
        out_specs=[pl.BlockSpec((n_mem, MIX_W), lambda b: (b, 0))] * 2,
        out_shape=[jax.ShapeDtypeStruct((batch * n_mem, MIX_W), BF16)] * 2,
        compiler_params=_cparams("parallel"),
        name="mem_kv",
    )(memf, gain, wkv, knorm)


def _ffn_kernel(x_ref, g_ref, wup_ref, cw_ref, cb_ref, wdn_ref, o_ref, carry_ref, xs_ref,
                *, tm, d_ff, n_chunk):
    @pl.when(pl.program_id(1) == 0)
    def _():
        carry_ref[...] = jnp.zeros_like(carry_ref)

    x = x_ref[...]
    h = _rms(x, g_ref[...]).astype(BF16)

    def conv_half(c0):
        u = jnp.dot(h, wup_ref[:, c0:c0 + n_chunk], preferred_element_type=F32)
        xs_ref[0:SUBLANES, :] = carry_ref[:, c0:c0 + n_chunk]
        xs_ref[SUBLANES:SUBLANES + tm, :] = u
        carry_ref[:, c0:c0 + n_chunk] = xs_ref[tm:tm + SUBLANES, :]
        out = jnp.broadcast_to(cb_ref[:, c0:c0 + n_chunk], (tm, n_chunk))
        for tap in range(FFN_CONV):
            off = SUBLANES - (FFN_CONV - 1) + tap
            out = out + cw_ref[tap:tap + 1, c0:c0 + n_chunk] * xs_ref[off:off + tm, :]
        return out

    acc = x
    for c0 in range(0, d_ff, n_chunk):
        a = conv_half(c0)
        b = conv_half(d_ff + c0)
        act = (_silu(a) * b).astype(BF16)
        acc = acc + jnp.dot(act, wdn_ref[c0:c0 + n_chunk, :], preferred_element_type=F32)
    o_ref[...] = acc


def _ffn(xf, gain, wup, conv_w, conv_b, wdn, batch, seq, tm, n_chunk):
    m, d = xf.shape
    d_ff = wdn.shape[0]
    nt = seq // tm
    return pl.pallas_call(
        functools.partial(_ffn_kernel, tm=tm, d_ff=d_ff, n_chunk=n_chunk),
        grid=(batch, nt),
        in_specs=[
            pl.BlockSpec((tm, d), lambda b, i: (b * nt + i, 0)),
            _const_spec((1, d)),
            _const_spec((d, 2 * d_ff)),
            _const_spec((FFN_CONV, 2 * d_ff)),
            _const_spec((1, 2 * d_ff)),
            _const_spec((d_ff, d)),
        ],
        out_specs=pl.BlockSpec((tm, d), lambda b, i: (b * nt + i, 0)),
        out_shape=jax.ShapeDtypeStruct((m, d), F32),
        scratch_shapes=[pltpu.VMEM((SUBLANES, 2 * d_ff), F32),
                        pltpu.VMEM((tm + SUBLANES, n_chunk), F32)],
        compiler_params=_cparams("parallel", "arbitrary"),
        name="conv_ffn",
    )(xf, gain, wup, conv_w, conv_b, wdn)


def _lane_row(values, lane0):
    return jnp.zeros((1, LANES), F32).at[0, lane0:lane0 + values.shape[0]].set(values.astype(F32))


def _pack_w_in(w):
    splits = (512, 512, 512, 4, 512, 512, 512, 4, 4, 512, 512, 512, 512)
    parts, off = [], 0
    for s in splits:
        parts.append(w[:, off:off + s])
        off += s
    gates = w[:, off:]
    fq, fk, fv, ff, gq, gk, gv, gb, ga, gz, sq, sk, sv = parts
    sq = sq * (HEAD_DIM ** -0.5)
    big =jnp.concatenate([fq, fk, fv, gq, gk, gv, gz, sq, sk, sv, gates], axis=1).astype(BF16)
    small = jnp.concatenate(
        [ff, gb, ga, jnp.zeros((w.shape[0], LANES - 3 * HEADS), w.dtype)], axis=1).astype(BF16)
    return big, small


def _tile(n, pref):
    return pref if n % pref == 0 else n


def kernel(x, mem, norm_mix, w_in, fox_fbias, fox_qnorm, fox_knorm, gdn_conv, gdn_a_log, gdn_dt_bias, gdn_onorm, gate_bias, w_oa, w_ob, w_oc, w_out, norm_xq, norm_mem, w_mq, w_mkv, mq_norm, mk_norm, w_mo, norm_ffn, w_up, ffn_conv, ffn_conv_b, w_down):
    batch, seq, d = x.shape
    n_mem = mem.shape[1]
    depth = w_in.shape[0]
    d_ff = w_down.shape[1]
    m = batch * seq
    assert d == 1024 and w_in.shape[2] == N_BIG + 3 * HEADS and seq % GDN_CHUNK == 0

    tm = _tile(seq, 512)
    tq = _tile(seq, 512)
    tk_sb = _tile(tq, 256)
    t_prep = _tile(seq, 256)
    tg = _tile(seq, 256)
    n_chunk = d_ff

    xf = x.reshape(m, d)
    memf = mem.reshape(batch * n_mem, d)
    row = lambda v: v.reshape(1, -1).astype(F32)
    for l in range(depth):
        w_big, w_small = _pack_w_in(w_in[l])
        y, g = _inproj(xf, row(norm_mix[l]), w_big, w_small, tm)

        q_aug, k_aug, c_ends = _fox_prep(y, g, _lane_row(fox_fbias[l], LANE_FF), row(fox_qnorm[l]),
                                         row(fox_knorm[l]), batch, seq, t_prep)
        step = tq // t_prep
        cend = c_ends[:, 0, LANE_FF:LANE_FF + HEADS].reshape(batch, seq // t_prep, HEADS)
        cend = cend[:, step - 1::step, :].transpose(0, 2, 1).reshape(batch * HEADS, seq // tq)
        ya = _fox_attn(cend, q_aug, k_aug, y, batch, seq, tq)
        yb = _gdn(y, g, gdn_conv[l].astype(F32), _lane_row(gdn_a_log[l], LANE_GA),
                  _lane_row(gdn_dt_bias[l], LANE_GA), row(gdn_onorm[l]), batch, seq, tg)
        yc = _sb_attn(y, batch, seq, tq, tk_sb)

        kmem, vmem = _mem_kv(memf, row(norm_mem[l]), w_mkv[l].astype(BF16), row(mk_norm[l]),
                             batch, n_mem)
        xf = _merge_mem(xf, ya, yb, yc, y, row(gate_bias[l]), w_oa[l].astype(BF16),
                        w_ob[l].astype(BF16), w_oc[l].astype(BF16), w_out[l].astype(BF16),
                        row(norm_xq[l]), w_mq[l].astype(BF16), row(mq_norm[l]), kmem, vmem,
                        w_mo[l].astype(BF16), batch, seq, n_mem, tm)

        xf = _ffn(xf, row(norm_ffn[l]), w_up[l].astype(BF16), ffn_conv[l].astype(F32),
                  row(ffn_conv_b[l]), w_down[l].astype(BF16), batch, seq, tm, n_chunk)
    return xf.reshape(batch, seq, d)
```

```python
import functools

import jax
import jax.numpy as jnp
from jax import lax
from jax.experimental import pallas as pl
from jax.experimental.pallas import tpu as pltpu

F32 = jnp.float32
BF16 = jnp.bfloat16
EPS = 1e-6
LOG2E = 1.4426950408889634
SKIP_EXP = -110.0
SKIP_EXP2 = -155.0
NORM_SLACK = 1.01

HEADS = 4
HEAD_DIM = 128
MIX_W = HEADS * HEAD_DIM
GDN_CHUNK = 64
GDN_CONV = 4
FFN_CONV = 3
LANES = 128
SUBLANES = 8
VMEM_LIMIT = 56 * 1024 * 1024

COL_FQ, COL_FK, COL_FV = 0, 512, 1024
COL_GQ, COL_GK, COL_GV, COL_GZ = 1536, 2048, 2560, 3072
COL_SQ, COL_SK, COL_SV = 3584, 4096, 4608
COL_GATES = 5120
N_BIG = 8192
LANE_FF, LANE_GB, LANE_GA = 0, 4, 8

NT_DIMS = (((1,), (1,)), ((), ()))
TN_DIMS = (((0,), (0,)), ((), ()))


def _cparams(*sem):
    return pltpu.CompilerParams(dimension_semantics=sem, vmem_limit_bytes=VMEM_LIMIT)


def _const_spec(shape):
    nd = len(shape)
    return pl.BlockSpec(shape, lambda *_: (0,) * nd, pipeline_mode=pl.Buffered(1))


def _rms(xf, gain):
    return xf * lax.rsqrt(jnp.mean(xf * xf, axis=-1, keepdims=True) + EPS) * gain


def _softplus(x):
    return jnp.maximum(x, 0.0) + jnp.log1p(jnp.exp(-jnp.abs(x)))


def _silu(x):
    h = 0.5 * x
    return h + h * jnp.tanh(h)


def _split3(x):
    a = x.astype(BF16)
    r = x - a.astype(F32)
    b = r.astype(BF16)
    c = (r - b.astype(F32)).astype(BF16)
    return a, b, c


def _lane_col(x, lane):
    idx = lax.broadcasted_iota(jnp.int32, x.shape, 1)
    return jnp.sum(jnp.where(idx == lane, x, 0.0), axis=-1, keepdims=True)


def _inproj_kernel(x_ref, g_ref, w_ref, ws_ref, y_ref, gs_ref, *, n_chunk):
    h = _rms(x_ref[...], g_ref[...]).astype(BF16)
    for c in range(0, N_BIG, n_chunk):
        y_ref[:, c:c + n_chunk] = jnp.dot(
            h, w_ref[:, c:c + n_chunk], preferred_element_type=F32).astype(BF16)
    gs_ref[...] = jnp.dot(h, ws_ref[...], preferred_element_type=F32)


def _inproj(xf, gain, w_big, w_small, tm):
    m, d = xf.shape
    return pl.pallas_call(
        functools.partial(_inproj_kernel, n_chunk=512),
        grid=(m // tm,),
        in_specs=[
            pl.BlockSpec((tm, d), lambda i: (i, 0)),
            _const_spec((1, d)),
            _const_spec((d, N_BIG)),
            _const_spec((d, LANES)),
        ],
        out_specs=[
            pl.BlockSpec((tm, N_BIG), lambda i: (i, 0)),
            pl.BlockSpec((tm, LANES), lambda i: (i, 0)),
        ],
        out_shape=[jax.ShapeDtypeStruct((m, N_BIG), BF16),
                   jax.ShapeDtypeStruct((m, LANES), F32)],
        compiler_params=_cparams("parallel"),
        name="inproj",
    )(xf, gain, w_big, w_small)


def _fox_prep_kernel(yq_ref, yk_ref, g_ref, fb_ref, qn_ref, kn_ref, qa_ref, ka_ref, ce_ref,
                     carry_ref, *, t):
    @pl.when(pl.program_id(1) == 0)
    def _():
        carry_ref[...] = jnp.zeros_like(carry_ref)

    lane = lax.broadcasted_iota(jnp.int32, (t, LANES), 1)
    x = g_ref[...] + fb_ref[...]
    logf = jnp.minimum(x, 0.0) - jnp.log1p(jnp.exp(-jnp.abs(x)))
    logf = jnp.where(lane < LANE_FF + HEADS, logf, 0.0)
    row = lax.broadcasted_iota(jnp.int32, (t, t), 0)
    col = lax.broadcasted_iota(jnp.int32, (t, t), 1)
    tri = jnp.where(row >= col, 1.0, 0.0).astype(BF16)
    l1, l2, l3 = _split3(logf)
    c_all = (jnp.dot(tri, l1, preferred_element_type=F32)
             + jnp.dot(tri, l2, preferred_element_type=F32)
             + jnp.dot(tri, l3, preferred_element_type=F32)) + carry_ref[0:1, :]
    carry_ref[0:1, :] = c_all[t - 1:t, :]
    ce_ref[0] = jnp.broadcast_to(c_all[t - 1:t, :] * LOG2E, (SUBLANES, LANES))

    scale = HEAD_DIM ** -0.5 * LOG2E
    for h in range(HEADS):
        sl = slice(h * HEAD_DIM, (h + 1) * HEAD_DIM)
        qn = _rms(yq_ref[:, sl].astype(F32), qn_ref[...]) * scale
        kn = _rms(yk_ref[:, sl].astype(F32), kn_ref[...])
        cb = jnp.broadcast_to(_lane_col(c_all, LANE_FF + h) * LOG2E, (t, LANES))
        c1, c2, c3 = (p.astype(F32) for p in _split3(cb))
        aux_q = jnp.where(lane == 0, c1, jnp.where(lane == 1, c2, jnp.where(
            lane == 2, c3, jnp.where(lane < 6, 1.0, 0.0))))
        aux_k = jnp.where(lane < 3, 1.0, jnp.where(lane == 3, -c1, jnp.where(
            lane == 4, -c2, jnp.where(lane == 5, -c3, 0.0))))
        base = 2 * h * HEAD_DIM
        qa_ref[:, base:base + HEAD_DIM] = qn.astype(BF16)
        qa_ref[:, base + HEAD_DIM:base + 2 * HEAD_DIM] = aux_q.astype(BF16)
        ka_ref[:, base:base + HEAD_DIM] = kn.astype(BF16)
        ka_ref[:, base + HEAD_DIM:base + 2 * HEAD_DIM] = aux_k.astype(BF16)


def _fox_prep(y, g, fb_row, qn, kn, batch, seq, t):
    m = batch * seq
    nt = seq // t
    rows = lambda b, i: b * nt + i
    return pl.pallas_call(
        functools.partial(_fox_prep_kernel, t=t),
        grid=(batch, nt),
        in_specs=[
            pl.BlockSpec((t, MIX_W), lambda b, i: (rows(b, i), COL_FQ // MIX_W)),
            pl.BlockSpec((t, MIX_W), lambda b, i: (rows(b, i), COL_FK // MIX_W)),
            pl.BlockSpec((t, LANES), lambda b, i: (rows(b, i), 0)),
            _const_spec((1, LANES)),
            _const_spec((1, HEAD_DIM)),
            _const_spec((1, HEAD_DIM)),
        ],
        out_specs=[
            pl.BlockSpec((t, 2 * MIX_W), lambda b, i: (rows(b, i), 0)),
            pl.BlockSpec((t, 2 * MIX_W), lambda b, i: (rows(b, i), 0)),
            pl.BlockSpec((1, SUBLANES, LANES), lambda b, i: (rows(b, i), 0, 0)),
        ],
        out_shape=[jax.ShapeDtypeStruct((m, 2 * MIX_W), BF16)] * 2
        + [jax.ShapeDtypeStruct((batch * nt, SUBLANES, LANES), F32)],
        scratch_shapes=[pltpu.VMEM((SUBLANES, LANES), F32)],
        compiler_params=_cparams("parallel", "arbitrary"),
        name="fox_prep",
    )(y, y, g, fb_row, qn, kn)


def _bounded_sweep(i, stage_a, stage_b, first_tile_fn):
    stage_a(i, 0, True)

    @pl.when(i == 0)
    def _():
        stage_b(i, 0, True)

    @pl.when(i > 0)
    def _():
        stage_a(i - 1, 1, False)
        stage_b(i, 0, True)
        lo = first_tile_fn()
        n = i - lo
        n_pairs = (n - 1) // 2

        def body(p, carry):
            t = i - 1 - 2 * p
            stage_a(t - 1, 0, False)
            stage_b(t, 1, False)
            stage_a(t - 2, 1, False)
            stage_b(t - 1, 0, False)
            return carry

        lax.fori_loop(0, n_pairs, body, 0)
        t = i - 1 - 2 * n_pairs

        @pl.when(jnp.logical_and(n >= 1, t == lo))
        def _():
            stage_b(t, 1, False)

        @pl.when(jnp.logical_and(n >= 1, t == lo + 1))
        def _():
            stage_a(lo, 0, False)
            stage_b(t, 1, False)
            stage_b(lo, 0, False)


def _checked_sweep(i, stage_a, stage_b, last_fn):
    stage_a(i, 0, True)

    @pl.when(i == 0)
    def _():
        stage_b(i, 0, True)

    @pl.when(i > 0)
    def _():
        last_0 = last_fn(0)

        @pl.when(last_0)
        def _():
            stage_b(i, 0, True)

        @pl.when(jnp.logical_not(last_0))
        def _():
            stage_a(i - 1, 1, False)
            stage_b(i, 0, True)

            def body(carry):
                t = carry[0]
                last_1 = jnp.logical_or(t == 0, last_fn(1))

                @pl.when(last_1)
                def _():
                    stage_b(t, 1, False)

                @pl.when(jnp.logical_not(last_1))
                def _():
                    stage_a(t - 1, 0, False)
                    stage_b(t, 1, False)

                last_2 = jnp.logical_or(t == 1, last_fn(0))
                more = jnp.logical_not(last_1)

                @pl.when(jnp.logical_and(more, last_2))
                def _():
                    stage_b(t - 1, 0, False)

                @pl.when(jnp.logical_and(more, jnp.logical_not(last_2)))
                def _():
                    stage_a(t - 2, 1, False)
                    stage_b(t - 1, 0, False)

                return t - 2, jnp.logical_or(last_1, last_2)

            lax.while_loop(lambda carry: jnp.logical_not(carry[1]), body, (i - 1, last_0))


def _max_sq_norm(ref, tq, seq, width):
    def chunk(c, best):
        x = ref[pl.ds(pl.multiple_of(c * tq, tq), tq), 0:width].astype(F32)
        return jnp.maximum(best, jnp.sum(x * x, axis=-1, keepdims=True))

    best = lax.fori_loop(0, seq // tq, chunk, jnp.zeros((tq, 1), F32))
    return jnp.max(best, axis=0, keepdims=True)


def _fox_attn_kernel(cend_ref, q_ref, k_ref, v_ref, o_ref, s_ref, m_ref, l_ref, acc_ref, zb_ref,
                     kmax_ref, *, tq, seq):
    @pl.when(pl.program_id(2) == 0)
    def _():
        kmax_ref[...] = jnp.broadcast_to(_max_sq_norm(k_ref, tq, seq, HEAD_DIM), kmax_ref.shape)

    m_ref[...] = jnp.full((tq, LANES), -jnp.inf, F32)
    l_ref[...] = jnp.zeros((tq, LANES), F32)
    acc_ref[...] = jnp.zeros((tq, HEAD_DIM), F32)

    qf = q_ref[:, 0:HEAD_DIM].astype(F32)
    aux = q_ref[:, HEAD_DIM:2 * HEAD_DIM].astype(F32)
    lane = lax.broadcasted_iota(jnp.int32, (tq, LANES), 1)
    c_q = jnp.sum(jnp.where(lane < 3, aux, 0.0), axis=-1, keepdims=True)
    q_sq = jnp.sum(qf * qf, axis=-1, keepdims=True)
    zb_ref[...] = jnp.sqrt(q_sq * kmax_ref[0:1, 0:1]) * NORM_SLACK + c_q

    def scores(j, slot, masked):
        del masked
        k = k_ref[pl.ds(pl.multiple_of(j * tq, tq), tq), :]
        s_ref[slot] = lax.dot_general(q_ref[...], k, NT_DIMS, preferred_element_type=F32)

    def softmax_pv(j, slot, masked):
        v = v_ref[pl.ds(pl.multiple_of(j * tq, tq), tq), :]
        s = s_ref[slot]
        if masked:
            row = lax.broadcasted_iota(jnp.int32, (tq, tq), 0)
            col = lax.broadcasted_iota(jnp.int32, (tq, tq), 1)
            s = jnp.where(col <= row, s, -jnp.inf)
        m_prev = m_ref[...]
        m_new = jnp.maximum(m_prev, jnp.max(s, axis=-1, keepdims=True))
        p = jnp.exp2(s - jnp.concatenate([m_new] * (tq // LANES), axis=1))
        alpha = jnp.exp2(m_prev - m_new)
        l_ref[...] = alpha * l_ref[...] + jnp.sum(p, axis=-1, keepdims=True)
        acc_ref[...] = alpha * acc_ref[...] + jnp.dot(p.astype(BF16), v, preferred_element_type=F32)
        m_ref[...] = m_new

    def first_tile_fn():
        slack = jnp.max(zb_ref[...] - m_ref[...])
        row = pl.program_id(0) * HEADS + pl.program_id(1)

        def count(j, lo):
            return lo + jnp.where(slack - cend_ref[row, j] < SKIP_EXP2, 1, 0)

        return lax.fori_loop(0, pl.program_id(2), count, 0)

    _bounded_sweep(pl.program_id(2), scores, softmax_pv, first_tile_fn)
    o_ref[...] = (acc_ref[...] / l_ref[...]).astype(BF16)


def _fox_attn(cend, q_aug, k_aug, y, batch, seq, tq):
    m = batch * seq
    nq = seq // tq
    return pl.pallas_call(
        functools.partial(_fox_attn_kernel, tq=tq, seq=seq),
        grid=(batch, HEADS, nq),
        in_specs=[
            pl.BlockSpec(memory_space=pltpu.SMEM),
            pl.BlockSpec((tq, 2 * HEAD_DIM), lambda b, h, i: (b * nq + i, h)),
            pl.BlockSpec((seq, 2 * HEAD_DIM), lambda b, h, i: (b, h)),
            pl.BlockSpec((seq, HEAD_DIM), lambda b, h, i: (b, COL_FV // HEAD_DIM + h)),
        ],
        out_specs=pl.BlockSpec((tq, HEAD_DIM), lambda b, h, i: (b * nq + i, h)),
        out_shape=jax.ShapeDtypeStruct((m, MIX_W), BF16),
        scratch_shapes=[pltpu.VMEM((2, tq, tq), F32),
                        pltpu.VMEM((tq, LANES), F32), pltpu.VMEM((tq, LANES), F32),
                        pltpu.VMEM((tq, HEAD_DIM), F32),
                        pltpu.VMEM((tq, 1), F32), pltpu.VMEM((SUBLANES, LANES), F32)],
        compiler_params=_cparams("parallel", "parallel", "arbitrary"),
        name="fox_attn",
    )(cend, q_aug, k_aug, y)


def _sb_attn_kernel(q_ref, k_ref, v_ref, o_ref, zw_ref, tot_ref, after_ref, acc_ref, zb_ref,
                    kmax_ref, *, tq, tk, seq):
    @pl.when(pl.program_id(2) == 0)
    def _():
        kmax_ref[...] = jnp.broadcast_to(_max_sq_norm(k_ref, tq, seq, HEAD_DIM), kmax_ref.shape)

    qf = q_ref[...].astype(F32)
    zb_ref[...] = jnp.sqrt(jnp.sum(qf * qf, axis=-1, keepdims=True) * kmax_ref[0:1, 0:1]) * NORM_SLACK

    n_sub = tq // tk
    row_k = lax.broadcasted_iota(jnp.int32, (tk, tk), 0)
    col_k = lax.broadcasted_iota(jnp.int32, (tk, tk), 1)
    neg_suffix = jnp.where(row_k >= col_k, -1.0, 0.0).astype(BF16)
    after_ref[...] = jnp.zeros((tq, 1), F32)
    acc_ref[...] = jnp.zeros((tq, HEAD_DIM), F32)

    def first_row(sub, masked):
        return sub * tk if masked else 0

    def valid_mask(sub):
        r0 = first_row(sub, True)
        row = lax.broadcasted_iota(jnp.int32, (tq - r0, tk), 0) + r0
        col = lax.broadcasted_iota(jnp.int32, (tq - r0, tk), 1) + sub * tk
        return col < row

    def logits(j, slot, masked):
        for sub in range(n_sub):
            r0 = first_row(sub, masked)
            k = k_ref[pl.ds(pl.multiple_of(j * tq + sub * tk, tk), tk), :]
            z = lax.dot_general(q_ref[r0:tq, :], k, NT_DIMS, preferred_element_type=F32)
            sp = jnp.maximum(z, 0.0) + jnp.log(1.0 + jnp.exp2(jnp.abs(z) * (-LOG2E)))
            if masked:
                sp = jnp.where(valid_mask(sub), sp, 0.0)
            within = jnp.dot(sp.astype(BF16), neg_suffix, preferred_element_type=F32)
            zw_ref[slot, sub, r0:tq, :] = z + within
            tot_ref[slot, sub, r0:tq, :] = within[:, 0:1]
            if r0:
                tot_ref[slot, sub, 0:r0, :] = jnp.zeros((r0, 1), F32)

    def weights_pv(j, slot, masked):
        for sub in reversed(range(n_sub)):
            r0 = first_row(sub, masked)
            v = v_ref[pl.ds(pl.multiple_of(j * tq + sub * tk, tk), tk), :]
            after = after_ref[r0:tq, :]
            e = zw_ref[slot, sub, r0:tq, :] + after
            if masked:
                e = jnp.where(valid_mask(sub), e, -jnp.inf)
            a = jnp.exp(e)
            acc_ref[r0:tq, :] += jnp.dot(a.astype(BF16), v, preferred_element_type=F32)
            after_ref[r0:tq, :] = after + tot_ref[slot, sub, r0:tq, :]

    def last_fn(slot):
        after = after_ref[...]
        for sub in reversed(range(n_sub)):
            after = after + tot_ref[slot, sub]
        return jnp.max(zb_ref[...] + after) < SKIP_EXP

    _checked_sweep(pl.program_id(2), logits, weights_pv, last_fn)
    o_ref[...] = acc_ref[...].astype(BF16)


def _sb_attn(y, batch, seq, tq, tk):
    m = batch * seq
    nq = seq // tq
    return pl.pallas_call(
        functools.partial(_sb_attn_kernel, tq=tq, tk=tk, seq=seq),
        grid=(batch, HEADS, nq),
        in_specs=[
            pl.BlockSpec((tq, HEAD_DIM), lambda b, h, i: (b * nq + i, COL_SQ // HEAD_DIM + h)),
            pl.BlockSpec((seq, HEAD_DIM), lambda b, h, i: (b, COL_SK // HEAD_DIM + h)),
            pl.BlockSpec((seq, HEAD_DIM), lambda b, h, i: (b, COL_SV // HEAD_DIM + h)),
        ],
        out_specs=pl.BlockSpec((tq, HEAD_DIM), lambda b, h, i: (b * nq + i, h)),
        out_shape=jax.ShapeDtypeStruct((m, MIX_W), BF16),
        scratch_shapes=[pltpu.VMEM((2, tq // tk, tq, tk), F32),
                        pltpu.VMEM((2, tq // tk, tq, 1), F32),
                        pltpu.VMEM((tq, 1), F32), pltpu.VMEM((tq, HEAD_DIM), F32),
                        pltpu.VMEM((tq, 1), F32), pltpu.VMEM((SUBLANES, LANES), F32)],
        compiler_params=_cparams("parallel", "parallel", "arbitrary"),
        name="sb_attn",
    )(y, y, y)


def _bdot(a, b):
    return jnp.dot(a.astype(BF16), b.astype(BF16), preferred_element_type=F32)


def _unit_lower_inverses(a_list, eye):
    c = eye.shape[0]
    xs = [eye - a for a in a_list]
    ps = [_bdot(a, a) for a in a_list]
    power = 2
    while 2 * power < c:
        xps = [_bdot(jnp.concatenate([x, p], axis=0), p) for x, p in zip(xs, ps)]
        xs = [x + xp[:c] for x, xp in zip(xs, xps)]
        ps = [xp[c:] for xp in xps]
        power *= 2
    return [x + _bdot(x, p) for x, p in zip(xs, ps)]


def _gdn_kernel(yq_ref, yk_ref, yv_ref, yz_ref, g_ref, cw_ref, alog_ref, bias_ref, on_ref,
                o_ref, state_ref, xs_ref, *, tg):
    @pl.when(pl.program_id(1) == 0)
    def _():
        state_ref[...] = jnp.zeros_like(state_ref)
        xs_ref[0:SUBLANES, :] = jnp.zeros((SUBLANES, 3 * MIX_W), F32)

    c = GDN_CHUNK
    xs_ref[SUBLANES:SUBLANES + tg, 0:MIX_W] = yq_ref[...].astype(F32)
    xs_ref[SUBLANES:SUBLANES + tg, MIX_W:2 * MIX_W] = yk_ref[...].astype(F32)
    xs_ref[SUBLANES:SUBLANES + tg, 2 * MIX_W:3 * MIX_W] = yv_ref[...].astype(F32)
    conv = jnp.zeros((tg, 3 * MIX_W), F32)
    for tap in range(GDN_CONV):
        off = SUBLANES - (GDN_CONV - 1) + tap
        conv = conv + cw_ref[tap:tap + 1, :] * xs_ref[off:off + tg, :]
    xs_ref[0:SUBLANES, :] = xs_ref[tg:tg + SUBLANES, :]
    qkv = _silu(conv)

    g_all = g_ref[...]
    beta_all = jax.nn.sigmoid(g_all)
    glog_all = -jnp.exp(alog_ref[...]) * _softplus(g_all + bias_ref[...])
    lane = lax.broadcasted_iota(jnp.int32, (tg, LANES), 1)
    glog_all = jnp.where((lane >= LANE_GA) & (lane < LANE_GA + HEADS), glog_all, 0.0)

    row = lax.broadcasted_iota(jnp.int32, (c, c), 0)
    col = lax.broadcasted_iota(jnp.int32, (c, c), 1)
    causal = row >= col
    strict = row > col
    eye = jnp.where(row == col, 1.0, 0.0)
    tril = jnp.where(causal, 1.0, 0.0).astype(BF16)

    items = [(n, h) for n in range(tg // c) for h in range(HEADS)]
    gc_alls, gc_rows = [], []
    for n in range(tg // c):
        g1, g2, g3 = _split3(glog_all[n * c:(n + 1) * c, :])
        gc_all = (jnp.dot(tril, g1, preferred_element_type=F32)
                  + jnp.dot(tril, g2, preferred_element_type=F32)
                  + jnp.dot(tril, g3, preferred_element_type=F32))
        gc_alls.append(gc_all)
        gc_rows.append(gc_all.T)

    pre = []
    for n, h in items:
        r0 = n * c
        cq = qkv[r0:r0 + c, h * HEAD_DIM:(h + 1) * HEAD_DIM]
        ck = qkv[r0:r0 + c, MIX_W + h * HEAD_DIM:MIX_W + (h + 1) * HEAD_DIM]
        v = qkv[r0:r0 + c, 2 * MIX_W + h * HEAD_DIM:2 * MIX_W + (h + 1) * HEAD_DIM]
        q = cq * lax.rsqrt(jnp.sum(cq * cq, axis=-1, keepdims=True) + EPS) * (HEAD_DIM ** -0.5)
        k = ck * lax.rsqrt(jnp.sum(ck * ck, axis=-1, keepdims=True) + EPS)
        beta = _lane_col(beta_all[r0:r0 + c, :], LANE_GB + h)
        gc = _lane_col(gc_alls[n], LANE_GA + h)
        gc_row = gc_rows[n][LANE_GA + h:LANE_GA + h + 1, :]
        g_last = gc[c - 1:c, :]
        decay = jnp.exp(jnp.where(causal, gc - gc_row, -jnp.inf))
        kb = k.astype(BF16)
        kk = lax.dot_general(kb, kb, NT_DIMS, preferred_element_type=F32)
        qk = lax.dot_general(q.astype(BF16), kb, NT_DIMS, preferred_element_type=F32)
        pre.append(dict(
            a_strict=jnp.where(strict, beta * kk * decay, 0.0),
            attn=jnp.where(causal, qk * decay, 0.0).astype(BF16),
            rhs=jnp.concatenate([v * beta, k * (beta * jnp.exp(gc))], axis=-1),
            q_dec=(q * jnp.exp(gc)).astype(BF16),
            k_tail=(k * jnp.exp(g_last - gc)).astype(BF16),
            s_dec=jnp.exp(g_last)))

    t_invs = _unit_lower_inverses([p["a_strict"] for p in pre], eye)
    sols = [_bdot(t, p["rhs"]) for t, p in zip(t_invs, pre)]

    for n in range(tg // c):
        r0 = n * c
        idx = [n * HEADS + h for h in range(HEADS)]
        states = [state_ref[h] for h in range(HEADS)]
        sbs = [s.astype(BF16) for s in states]
        vnbs = [(sols[i][:, :HEAD_DIM] - jnp.dot(sols[i][:, HEAD_DIM:].astype(BF16), sb,
                                                 preferred_element_type=F32)).astype(BF16)
                for i, sb in zip(idx, sbs)]
        outs = [jnp.dot(pre[i]["q_dec"], sb, preferred_element_type=F32)
                + jnp.dot(pre[i]["attn"], vnb, preferred_element_type=F32)
                for i, sb, vnb in zip(idx, sbs, vnbs)]
        for h in range(HEADS):
            i = idx[h]
            state_ref[h] = states[h] * pre[i]["s_dec"] + lax.dot_general(
                pre[i]["k_tail"], vnbs[h], TN_DIMS, preferred_element_type=F32)
            sl = slice(h * HEAD_DIM, (h + 1) * HEAD_DIM)
            z = yz_ref[r0:r0 + c, sl].astype(F32)
            o_ref[r0:r0 + c, sl] = (_rms(outs[h], on_ref[...]) * _silu(z)).astype(BF16)


def _gdn(y, g, conv_w, alog_row, bias_row, onorm, batch, seq, tg):
    m = batch * seq
    nt = seq // tg
    rows = lambda b, i: b * nt + i
    ycol = lambda off: pl.BlockSpec((tg, MIX_W), lambda b, i: (rows(b, i), off // MIX_W))
    return pl.pallas_call(
        functools.partial(_gdn_kernel, tg=tg),
        grid=(batch, nt),
        in_specs=[
            ycol(COL_GQ), ycol(COL_GK), ycol(COL_GV), ycol(COL_GZ),
            pl.BlockSpec((tg, LANES), lambda b, i: (rows(b, i), 0)),
            _const_spec((GDN_CONV, 3 * MIX_W)),
            _const_spec((1, LANES)),
            _const_spec((1, LANES)),
            _const_spec((1, HEAD_DIM)),
        ],
        out_specs=pl.BlockSpec((tg, MIX_W), lambda b, i: (rows(b, i), 0)),
        out_shape=jax.ShapeDtypeStruct((m, MIX_W), BF16),
        scratch_shapes=[pltpu.VMEM((HEADS, HEAD_DIM, HEAD_DIM), F32),
                        pltpu.VMEM((tg + SUBLANES, 3 * MIX_W), F32)],
        compiler_params=_cparams("parallel", "arbitrary"),
        name="gdn",
    )(y, y, y, y, g, conv_w, alog_row, bias_row, onorm)


def _merge_mem_kernel(x_ref, ya_ref, yb_ref, yc_ref, g0_ref, g1_ref, g2_ref, gb_ref,
                      woa_ref, wob_ref, woc_ref, wout_ref,
                      gq_ref, wq_ref, qn_ref, k_ref, v_ref, wo_ref, o_ref, *, d):
    def branch(y_ref, w_ref, gate_ref, idx):
        gate = jax.nn.sigmoid(gate_ref[...].astype(F32) + gb_ref[:, idx * d:(idx + 1) * d])
        return gate * jnp.dot(y_ref[...], w_ref[...], preferred_element_type=F32)

    mixed = (branch(ya_ref, woa_ref, g0_ref, 0) + branch(yb_ref, wob_ref, g1_ref, 1)
             + branch(yc_ref, woc_ref, g2_ref, 2))
    x = x_ref[...] + jnp.dot(mixed.astype(BF16), wout_ref[...], preferred_element_type=F32)

    hq = _rms(x, gq_ref[...]).astype(BF16)
    q_all = jnp.dot(hq, wq_ref[...], preferred_element_type=F32)
    outs = []
    for h in range(HEADS):
        sl = slice(h * HEAD_DIM, (h + 1) * HEAD_DIM)
        q = (_rms(q_all[:, sl], qn_ref[...]) * (HEAD_DIM ** -0.5)).astype(BF16)
        s = lax.dot_general(q, k_ref[:, sl], NT_DIMS, preferred_element_type=F32)
        p = jnp.exp(s - jnp.max(s, axis=-1, keepdims=True))
        p = p / jnp.sum(p, axis=-1, keepdims=True)
        outs.append(jnp.dot(p.astype(BF16), v_ref[:, sl], preferred_element_type=F32).astype(BF16))
    o = jnp.concatenate(outs, axis=-1)
    o_ref[...] = x + jnp.dot(o, wo_ref[...], preferred_element_type=F32)


def _merge_mem(xf, ya, yb, yc, y, gate_bias, woa, wob, woc, wout, gain_q, wq, qnorm, kmem, vmem,
               wo, batch, seq, n_mem, tm):
    m, d = xf.shape
    nt = seq // tm
    rowblk = lambda w, cb: pl.BlockSpec((tm, w), lambda b, i: (b * nt + i, cb))
    return pl.pallas_call(
        functools.partial(_merge_mem_kernel, d=d),
        grid=(batch, nt),
        in_specs=[
            rowblk(d, 0), rowblk(MIX_W, 0), rowblk(MIX_W, 0), rowblk(MIX_W, 0),
            rowblk(d, COL_GATES // d), rowblk(d, COL_GATES // d + 1), rowblk(d, COL_GATES // d + 2),
            _const_spec((1, 3 * d)),
            _const_spec((MIX_W, d)), _const_spec((MIX_W, d)), _const_spec((MIX_W, d)),
            _const_spec((d, d)),
            _const_spec((1, d)),
            _const_spec((d, MIX_W)),
            _const_spec((1, HEAD_DIM)),
            pl.BlockSpec((n_mem, MIX_W), lambda b, i: (b, 0)),
            pl.BlockSpec((n_mem, MIX_W), lambda b, i: (b, 0)),
            _const_spec((MIX_W, d)),
        ],
        out_specs=rowblk(d, 0),
        out_shape=jax.ShapeDtypeStruct((m, d), F32),
        compiler_params=_cparams("parallel", "parallel"),
        name="merge_mem",
    )(xf, ya, yb, yc, y, y, y, gate_bias, woa, wob, woc, wout, gain_q, wq, qnorm, kmem, vmem, wo)


def _mem_kv_kernel(m_ref, g_ref, wkv_ref, kn_ref, k_ref, v_ref):
    mn = _rms(m_ref[...], g_ref[...]).astype(BF16)
    kv = jnp.dot(mn, wkv_ref[...], preferred_element_type=F32)
    for h in range(HEADS):
        sl = slice(h * HEAD_DIM, (h + 1) * HEAD_DIM)
        k_ref[:, sl] = _rms(kv[:, sl], kn_ref[...]).astype(BF16)
    v_ref[...] = kv[:, MIX_W:].astype(BF16)


def _mem_kv(memf, gain, wkv, knorm, batch, n_mem):
    d = memf.shape[1]
    return pl.pallas_call(
        _mem_kv_kernel,
        grid=(batch,),
        in_specs=[
            pl.BlockSpec((n_mem, d), lambda b: (b, 0)),
            _const_spec((1, d)),
            _const_spec((d, 2 * MIX_W)),
            _const_spec((1, HEAD_DIM)),
        ],
        out_specs=[pl.BlockSpec((n_mem, MIX_W), lambda b: (b, 0))] * 2,
        out_shape=[jax.ShapeDtypeStruct((batch * n_mem, MIX_W), BF16)] * 2,
        compiler_params=_cparams("parallel"),
        name="mem_kv",
    )(memf, gain, wkv, knorm)


def _ffn_kernel(x_ref, g_ref, wup_ref, cw_ref, cb_ref, wdn_ref, o_ref, carry_ref, xs_ref,
                *, tm, d_ff, n_chunk):
    @pl.when(pl.program_id(1) == 0)
    def _():
        carry_ref[...] = jnp.zeros_like(carry_ref)

    x = x_ref[...]
    h = _rms(x, g_ref[...]).astype(BF16)

    def conv_half(c0):
        u = jnp.dot(h, wup_ref[:, c0:c0 + n_chunk], preferred_element_type=F32)
        xs_ref[0:SUBLANES, :] = carry_ref[:, c0:c0 + n_chunk]
        xs_ref[SUBLANES:SUBLANES + tm, :] = u
        carry_ref[:, c0:c0 + n_chunk] = xs_ref[tm:tm + SUBLANES, :]
        out = jnp.broadcast_to(cb_ref[:, c0:c0 + n_chunk], (tm, n_chunk))
        for tap in range(FFN_CONV):
            off = SUBLANES - (FFN_CONV - 1) + tap
            out = out + cw_ref[tap:tap + 1, c0:c0 + n_chunk] * xs_ref[off:off + tm, :]
        return out

    acc = x
    for c0 in range(0, d_ff, n_chunk):
        a = conv_half(c0)
        b = conv_half(d_ff + c0)
        act = (_silu(a) * b).astype(BF16)
        acc = acc + jnp.dot(act, wdn_ref[c0:c0 + n_chunk, :], preferred_element_type=F32)
    o_ref[...] = acc


def _ffn(xf, gain, wup, conv_w, conv_b, wdn, batch, seq, tm, n_chunk):
    m, d = xf.shape
    d_ff = wdn.shape[0]
    nt = seq // tm
    return pl.pallas_call(
        functools.partial(_ffn_kernel, tm=tm, d_ff=d_ff, n_chunk=n_chunk),
        grid=(batch, nt),
        in_specs=[
            pl.BlockSpec((tm, d), lambda b, i: (b * nt + i, 0)),
            _const_spec((1, d)),
            _const_spec((d, 2 * d_ff)),
            _const_spec((FFN_CONV, 2 * d_ff)),
            _const_spec((1, 2 * d_ff)),
            _const_spec((d_ff, d)),
        ],
        out_specs=pl.BlockSpec((tm, d), lambda b, i: (b * nt + i, 0)),
        out_shape=jax.ShapeDtypeStruct((m, d), F32),
        scratch_shapes=[pltpu.VMEM((SUBLANES, 2 * d_ff), F32),
                        pltpu.VMEM((tm + SUBLANES, n_chunk), F32)],
        compiler_params=_cparams("parallel", "arbitrary"),
        name="conv_ffn",
    )(xf, gain, wup, conv_w, conv_b, wdn)


def _lane_row(values, lane0):
    return jnp.zeros((1, LANES), F32).at[0, lane0:lane0 + values.shape[0]].set(values.astype(F32))


def _pack_w_in(w):
    splits = (512, 512, 512, 4, 512, 512, 512, 4, 4, 512, 512, 512, 512)
    parts, off = [], 0
    for s in splits:
        parts.append(w[:, off:off + s])
        off += s
    gates = w[:, off:]
    fq, fk, fv, ff, gq, gk, gv, gb, ga, gz, sq, sk, sv = parts
    sq = sq * (HEAD_DIM ** -0.5)
    big =jnp.concatenate([fq, fk, fv, gq, gk, gv, gz, sq, sk, sv, gates], axis=1).astype(BF16)
    small = jnp.concatenate(
        [ff, gb, ga, jnp.zeros((w.shape[0], LANES - 3 * HEADS), w.dtype)], axis=1).astype(BF16)
    return big, small


def _tile(n, pref):
    return pref if n % pref == 0 else n


def kernel(x, mem, norm_mix, w_in, fox_fbias, fox_qnorm, fox_knorm, gdn_conv, gdn_a_log, gdn_dt_bias, gdn_onorm, gate_bias, w_oa, w_ob, w_oc, w_out, norm_xq, norm_mem, w_mq, w_mkv, mq_norm, mk_norm, w_mo, norm_ffn, w_up, ffn_conv, ffn_conv_b, w_down):
    batch, seq, d = x.shape
    n_mem = mem.shape[1]
    depth = w_in.shape[0]
    d_ff = w_down.shape[1]
    m = batch * seq
    assert d == 1024 and w_in.shape[2] == N_BIG + 3 * HEADS and seq % GDN_CHUNK == 0

    tm = _tile(seq, 512)
    tq = _tile(seq, 512)
    tk_sb = _tile(tq, 256)
    t_prep = _tile(seq, 256)
    tg = _tile(seq, 256)
    n_chunk = d_ff

    xf = x.reshape(m, d)
    memf = mem.reshape(batch * n_mem, d)
    row = lambda v: v.reshape(1, -1).astype(F32)
    for l in range(depth):
        w_big, w_small = _pack_w_in(w_in[l])
        y, g = _inproj(xf, row(norm_mix[l]), w_big, w_small, tm)

        q_aug, k_aug, c_ends = _fox_prep(y, g, _lane_row(fox_fbias[l], LANE_FF), row(fox_qnorm[l]),
                                         row(fox_knorm[l]), batch, seq, t_prep)
        step = tq // t_prep
        cend = c_ends[:, 0, LANE_FF:LANE_FF + HEADS].reshape(batch, seq // t_prep, HEADS)
        cend = cend[:, step - 1::step, :].transpose(0, 2, 1).reshape(batch * HEADS, seq // tq)
        ya = _fox_attn(cend, q_aug, k_aug, y, batch, seq, tq)
        yb = _gdn(y, g, gdn_conv[l].astype(F32), _lane_row(gdn_a_log[l], LANE_GA),
                  _lane_row(gdn_dt_bias[l], LANE_GA), row(gdn_onorm[l]), batch, seq, tg)
        yc = _sb_attn(y, batch, seq, tq, tk_sb)

        kmem, vmem = _mem_kv(memf, row(norm_mem[l]), w_mkv[l].astype(BF16), row(mk_norm[l]),
                             batch, n_mem)
        xf = _merge_mem(xf, ya, yb, yc, y, row(gate_bias[l]), w_oa[l].astype(BF16),
                        w_ob[l].astype(BF16), w_oc[l].astype(BF16), w_out[l].astype(BF16),
                        row(norm_xq[l]), w_mq[l].astype(BF16), row(mq_norm[l]), kmem, vmem,
                        w_mo[l].astype(BF16), batch, seq, n_mem, tm)

        xf = _ffn(xf, row(norm_ffn[l]), w_up[l].astype(BF16), ffn_conv[l].astype(F32),
                  row(ffn_conv_b[l]), w_down[l].astype(BF16), batch, seq, tm, n_chunk)
    return xf.reshape(batch, seq, d)
```
